```python
import jax, jax.numpy as jnp
from jax import lax
import numpy as np

D_MODEL = 1024
BATCH = 8
SEQ = 4096
DEPTH = 2
DEC_BATCH = 16
DEC_SEQ = 64
PAST_LEN = 1024

CHUNK = 64
POOL_WINDOWS = (2, 4, 8, 16)
POOL_GROUPS = 4
POOL_GROUP_DIM = 128
POOL_WIDTH = POOL_GROUPS * POOL_GROUP_DIM
POOL_OUT_GROUP = D_MODEL // POOL_GROUPS
POOL_HIST = max(POOL_WINDOWS) - 1
CONV_WIDTH = D_MODEL // 2
CONV_K = 3
N_BRANCH = 2
IN_WIDTH = POOL_WIDTH + 3 * CONV_WIDTH + N_BRANCH * D_MODEL
D_FF = 2816
EPS = 1e-6

kernel_name = "hybrid_pool_shortconv_convffn_stream_step"


def rmsnorm(x, g):
    xf = x.astype(jnp.float32)
    y = xf * lax.rsqrt(jnp.mean(xf * xf, axis=-1, keepdims=True) + EPS)
    return (y * g.astype(jnp.float32)).astype(x.dtype)


def causal_dwconv(u, buf, w):
    S = u.shape[1]
    ext = jnp.concatenate([buf.astype(u.dtype), u], axis=1)
    y = sum(ext[:, k:k + S] * w[k] for k in range(CONV_K))
    return y, ext[:, -(CONV_K - 1):]


def pool_mixer(u, buf, pos, w_map, scale):
    B, S, _ = u.shape
    ext = jnp.concatenate([buf.astype(u.dtype), u], axis=1)
    cs = jnp.cumsum(ext.astype(jnp.float32), axis=1)
    cs = jnp.pad(cs, ((0, 0), (1, 0), (0, 0)))
    outs = []
    for gi, win in enumerate(POOL_WINDOWS):
        sl = slice(gi * POOL_GROUP_DIM, (gi + 1) * POOL_GROUP_DIM)
        hi = cs[:, POOL_HIST + 1:POOL_HIST + 1 + S, sl]
        lo = cs[:, POOL_HIST + 1 - win:POOL_HIST + 1 - win + S, sl]
        cnt = jnp.minimum(win, pos + 1).astype(jnp.float32)[None, :, None]
        outs.append((hi - lo) / cnt)
    pooled = jnp.stack(outs, axis=2)
    mixed = (pooled - u.reshape(B, S, POOL_GROUPS, POOL_GROUP_DIM).astype(jnp.float32)).astype(u.dtype)
    y = jnp.einsum('bsgc,gcd->bsgd', mixed, w_map).reshape(B, S, D_MODEL)
    return y * scale, ext[:, -POOL_HIST:]


def layer(x, pool_buf, conv_buf, ffn_buf, pos, norm_mix_g, w_in, b_gate, w_pool_map,
          pool_scale, conv_w, w_conv_out, w_o, norm_ffn_g, w_up, ffn_conv_w, ffn_conv_b, w_down):
    h = rmsnorm(x, norm_mix_g)
    proj = jnp.einsum('bsd,de->bse', h, w_in)
    o = 0
    u_pool = proj[..., o:o + POOL_WIDTH]; o += POOL_WIDTH
    gb = proj[..., o:o + CONV_WIDTH]; o += CONV_WIDTH
    gc = proj[..., o:o + CONV_WIDTH]; o += CONV_WIDTH
    v = proj[..., o:o + CONV_WIDTH]; o += CONV_WIDTH
    gate_logits = proj[..., o:o + N_BRANCH * D_MODEL] + b_gate
    y_a, new_pool = pool_mixer(u_pool, pool_buf, pos, w_pool_map, pool_scale)
    cv, new_conv = causal_dwconv(gc * v, conv_buf, conv_w)
    y_b = jnp.einsum('bsc,cd->bsd', gb * cv, w_conv_out)
    gates = jax.nn.sigmoid(gate_logits.astype(jnp.float32)).astype(x.dtype)
    merged = gates[..., :D_MODEL] * y_a + gates[..., D_MODEL:] * y_b
    x = x + jnp.einsum('bsd,de->bse', merged, w_o)
    h = rmsnorm(x, norm_ffn_g)
    up = jnp.einsum('bsd,df->bsf', h, w_up)
    upc, new_ffn = causal_dwconv(up, ffn_buf, ffn_conv_w)
    upc = upc + ffn_conv_b
    act = jax.nn.silu(upc[..., D_FF:]) * upc[..., :D_FF]
    x = x + jnp.einsum('bsf,fd->bsd', act, w_down)
    return x, new_pool, new_conv, new_ffn


def setup_inputs(seed: int = 0) -> dict:
    key = jax.random.key(seed)
    ks = jax.random.split(key, 24)
    f32 = jnp.float32
    nrm = lambda k, shape, s: (jax.random.normal(k, shape, f32) * s)
    return {
        "x_prompt": nrm(ks[0], (BATCH, SEQ, D_MODEL), 1.0),
        "x_sample": nrm(ks[1], (DEC_BATCH, DEC_SEQ, D_MODEL), 1.0),
        "state_pool": nrm(ks[2], (DEPTH, DEC_BATCH, POOL_HIST, POOL_WIDTH), 1.0),
        "state_conv": nrm(ks[3], (DEPTH, DEC_BATCH, CONV_K - 1, CONV_WIDTH), 1.0),
        "state_ffn": nrm(ks[4], (DEPTH, DEC_BATCH, CONV_K - 1, 2 * D_FF), 1.0),
        "norm_mix_g": 1.0 + nrm(ks[5], (DEPTH, D_MODEL), 0.05),
        "w_in": nrm(ks[6], (DEPTH, D_MODEL, IN_WIDTH), D_MODEL ** -0.5),
        "b_gate": nrm(ks[7], (DEPTH, N_BRANCH * D_MODEL), 0.05),
        "w_pool_map": nrm(ks[8], (DEPTH, POOL_GROUPS, POOL_GROUP_DIM, POOL_OUT_GROUP), POOL_GROUP_DIM ** -0.5),
        "pool_scale": 1.0 + nrm(ks[9], (DEPTH, D_MODEL), 0.05),
        "conv_w": nrm(ks[10], (DEPTH, CONV_K, CONV_WIDTH), CONV_K ** -0.5),
        "w_conv_out": nrm(ks[11], (DEPTH, CONV_WIDTH, D_MODEL), CONV_WIDTH ** -0.5),
        "w_o": nrm(ks[12], (DEPTH, D_MODEL, D_MODEL), D_MODEL ** -0.5),
        "norm_ffn_g": 1.0 + nrm(ks[13], (DEPTH, D_MODEL), 0.05),
        "w_up": nrm(ks[14], (DEPTH, D_MODEL, 2 * D_FF), D_MODEL ** -0.5),
        "ffn_conv_w": nrm(ks[15], (DEPTH, CONV_K, 2 * D_FF), CONV_K ** -0.5),
        "ffn_conv_b": nrm(ks[16], (DEPTH, 2 * D_FF), 0.02),
        "w_down": nrm(ks[17], (DEPTH, D_FF, D_MODEL), D_FF ** -0.5),
        "final_norm_g": 1.0 + nrm(ks[18], (D_MODEL,), 0.05),
    }


def reference(x_prompt, x_sample, state_pool, state_conv, state_ffn, norm_mix_g, w_in, b_gate,
              w_pool_map, pool_scale, conv_w, w_conv_out, w_o, norm_ffn_g, w_up, ffn_conv_w,
              ffn_conv_b, w_down, final_norm_g):
    S_p = x_prompt.shape[1]
    S_s = x_sample.shape[1]
    pos_p = jnp.arange(S_p, dtype=jnp.int32)
    pos_s = PAST_LEN + jnp.arange(S_s, dtype=jnp.int32)
    zp_pool = jnp.zeros((x_prompt.shape[0], POOL_HIST, POOL_WIDTH), x_prompt.dtype)
    zp_conv = jnp.zeros((x_prompt.shape[0], CONV_K - 1, CONV_WIDTH), x_prompt.dtype)
    zp_ffn = jnp.zeros((x_prompt.shape[0], CONV_K - 1, 2 * D_FF), x_prompt.dtype)

    xp, xs = x_prompt, x_sample
    pp, cp, fp, ps, cs_, fs = [], [], [], [], [], []
    for l in range(DEPTH):
        params = (norm_mix_g[l], w_in[l], b_gate[l], w_pool_map[l], pool_scale[l], conv_w[l],
                  w_conv_out[l], w_o[l], norm_ffn_g[l], w_up[l], ffn_conv_w[l], ffn_conv_b[l], w_down[l])
        xp, a, b, c = layer(xp, zp_pool, zp_conv, zp_ffn, pos_p, *params)
        pp.append(a); cp.append(b); fp.append(c)
        xs, a, b, c = layer(xs, state_pool[l], state_conv[l], state_ffn[l], pos_s, *params)
        ps.append(a); cs_.append(b); fs.append(c)

    y_prompt = rmsnorm(xp, final_norm_g)
    y_sample = rmsnorm(xs, final_norm_g)
    return (y_prompt, y_sample, jnp.stack(pp), jnp.stack(cp), jnp.stack(fp),
            jnp.stack(ps), jnp.stack(cs_), jnp.stack(fs))
```

```python
import functools

import jax
import jax.numpy as jnp
from jax import lax
from jax.experimental import pallas as pl
from jax.experimental.pallas import tpu as pltpu

D_MODEL = 1024
POOL_WINDOWS = (2, 4, 8, 16)
POOL_GROUP_DIM = 128
POOL_WIDTH = 512
POOL_OUT_GROUP = 256
POOL_HIST = 15
CONV_WIDTH = 512
CONV_K = 3
D_FF = 2816
PAST_LEN = 1024
EPS = 1e-6

POOL_HIST_ROWS = 16
CONV_HIST_ROWS = 8
FFN_CHUNK = 256
N_FFN_CHUNKS = D_FF // FFN_CHUNK
PROMPT_TILE = 512
VMEM_LIMIT_BYTES = 56 * 1024 * 1024

_BF16 = jnp.bfloat16
_F32 = jnp.float32


def _dot(a, b):
    return jnp.dot(a, b, preferred_element_type=_F32)


def _rmsnorm(x, g):
    y = x * lax.rsqrt(jnp.mean(x * x, axis=-1, keepdims=True) + EPS)
    return y * g


def _causal_conv3(hist, u, w_ref):
    ext = jnp.concatenate([hist, u], axis=0)
    prev2 = pltpu.roll(ext, 2, axis=0)[CONV_HIST_ROWS:]
    prev1 = pltpu.roll(ext, 1, axis=0)[CONV_HIST_ROWS:]
    return prev2 * w_ref[0:1, :] + prev1 * w_ref[1:2, :] + u * w_ref[2:3, :]


def _mixer_kernel(pos0, x_ref, sp_ref, sc_ref, g_ref, w_in_ref, bg_ref, wmap_ref, pscale_ref,
                  convw_ref, wco_ref, wo_ref, out_ref, pool_out_ref, conv_out_ref,
                  pool_hist, conv_hist):
    t = pl.program_id(1)
    rows = x_ref.shape[1]

    @pl.when(t == 0)
    def _():
        pool_hist[...] = sp_ref[0]
        conv_hist[...] = sc_ref[0]

    x = x_ref[0]
    hb = _rmsnorm(x, g_ref[...]).astype(_BF16)

    u = _dot(hb, w_in_ref[:, 0:POOL_WIDTH])
    ext = jnp.concatenate([pool_hist[...], u], axis=0)
    pos = pos0 + t * rows + lax.broadcasted_iota(jnp.int32, (rows, POOL_GROUP_DIM), 0)
    ya_parts = []
    for gi, win in enumerate(POOL_WINDOWS):
        lanes = slice(gi * POOL_GROUP_DIM, (gi + 1) * POOL_GROUP_DIM)
        s = ext[:, lanes]
        k = 1
        while k < win:
            s = s + pltpu.roll(s, k, axis=0)
            k *= 2
        cnt = jnp.minimum(win, pos + 1).astype(_F32)
        mixed = s[POOL_HIST_ROWS:] / cnt - u[:, lanes]
        ya_parts.append(_dot(mixed.astype(_BF16), wmap_ref[gi]))
    y_a = jnp.concatenate(ya_parts, axis=1) * pscale_ref[...]
    pool_hist[...] = u[rows - POOL_HIST_ROWS:]
    pool_out_ref[0] = u[rows - POOL_HIST_ROWS:]

    o = POOL_WIDTH
    gb = _dot(hb, w_in_ref[:, o:o + CONV_WIDTH]); o += CONV_WIDTH
    gc = _dot(hb, w_in_ref[:, o:o + CONV_WIDTH]); o += CONV_WIDTH
    v = _dot(hb, w_in_ref[:, o:o + CONV_WIDTH]); o += CONV_WIDTH
    p = gc * v
    cv = _causal_conv3(conv_hist[...], p, convw_ref)
    conv_hist[...] = p[rows - CONV_HIST_ROWS:]
    conv_out_ref[0] = p[rows - CONV_HIST_ROWS:]
    y_b = _dot((gb * cv).astype(_BF16), wco_ref[...])

    la = _dot(hb, w_in_ref[:, o:o + D_MODEL]) + bg_ref[:, 0:D_MODEL]
    merged = jax.nn.sigmoid(la) * y_a
    lb = _dot(hb, w_in_ref[:, o + D_MODEL:o + 2 * D_MODEL]) + bg_ref[:, D_MODEL:2 * D_MODEL]
    merged = merged + jax.nn.sigmoid(lb) * y_b
    out_ref[0] = x + _dot(merged.astype(_BF16), wo_ref[...])


def _ffn_kernel(final, x_ref, sf_ref, g_ref, wup_ref, cw_ref, cb_ref, wdn_ref, fg_ref,
                out_ref, ffn_out_ref, hist, act_scr):
    t = pl.program_id(1)
    rows = x_ref.shape[1]

    @pl.when(t == 0)
    def _():
        hist[...] = sf_ref[0]

    x = x_ref[0]
    hb = _rmsnorm(x, g_ref[...]).astype(_BF16)
    width = 2 * FFN_CHUNK
    for c in range(N_FFN_CHUNKS):
        cols = slice(c * width, (c + 1) * width)
        up = _dot(hb, wup_ref[:, cols])
        upc = _causal_conv3(hist[:, cols], up, cw_ref.at[:, cols]) + cb_ref[:, cols]
        hist[:, cols] = up[rows - CONV_HIST_ROWS:]
        act = jax.nn.silu(upc[:, FFN_CHUNK:]) * upc[:, :FFN_CHUNK]
        act_scr[:, c * FFN_CHUNK:(c + 1) * FFN_CHUNK] = act.astype(_BF16)
    ffn_out_ref[0] = hist[...]
    y = x + _dot(act_scr[...], wdn_ref[...])
    if final:
        y = _rmsnorm(y, fg_ref[...])
    out_ref[0] = y


def _resident(shape):
    zeros = (0,) * len(shape)
    return pl.BlockSpec(shape, lambda b, t: zeros, pipeline_mode=pl.Buffered(1))


def _per_seq(shape):
    return pl.BlockSpec((1,) + shape, lambda b, t: (b, 0, 0))


def _mixer_call(x, sp, sc, prm, pos0, tile):
    nseq, slen, _ = x.shape
    x_spec = pl.BlockSpec((1, tile, D_MODEL), lambda b, t: (b, t, 0))
    weights = (prm["g_mix"], prm["w_in"], prm["b_gate"], prm["w_map"], prm["pool_scale"],
               prm["conv_w"], prm["w_co"], prm["w_o"])
    return pl.pallas_call(
        functools.partial(_mixer_kernel, pos0),
        grid=(nseq, slen // tile),
        in_specs=[x_spec, _per_seq((POOL_HIST_ROWS, POOL_WIDTH)),
                  _per_seq((CONV_HIST_ROWS, CONV_WIDTH))] + [_resident(w.shape) for w in weights],
        out_specs=[x_spec, _per_seq((POOL_HIST_ROWS, POOL_WIDTH)),
                   _per_seq((CONV_HIST_ROWS, CONV_WIDTH))],
        out_shape=[jax.ShapeDtypeStruct(x.shape, _F32),
                   jax.ShapeDtypeStruct((nseq, POOL_HIST_ROWS, POOL_WIDTH), _F32),
                   jax.ShapeDtypeStruct((nseq, CONV_HIST_ROWS, CONV_WIDTH), _F32)],
        scratch_shapes=[pltpu.VMEM((POOL_HIST_ROWS, POOL_WIDTH), _F32),
                        pltpu.VMEM((CONV_HIST_ROWS, CONV_WIDTH), _F32)],
        compiler_params=pltpu.CompilerParams(
            dimension_semantics=("arbitrary", "arbitrary"), vmem_limit_bytes=VMEM_LIMIT_BYTES),
        name="mixer",
    )(x, sp, sc, *weights)


def _ffn_call(x, sf, prm, final_g, final, tile):
    nseq, slen, _ = x.shape
    x_spec = pl.BlockSpec((1, tile, D_MODEL), lambda b, t: (b, t, 0))
    weights = (prm["g_ffn"], prm["w_up"], prm["ffn_w"], prm["ffn_b"], prm["w_down"], final_g)
    return pl.pallas_call(
        functools.partial(_ffn_kernel, final),
        grid=(nseq, slen // tile),
        in_specs=[x_spec, _per_seq((CONV_HIST_ROWS, 2 * D_FF))] + [_resident(w.shape) for w in weights],
        out_specs=[x_spec, _per_seq((CONV_HIST_ROWS, 2 * D_FF))],
        out_shape=[jax.ShapeDtypeStruct(x.shape, _F32),
                   jax.ShapeDtypeStruct((nseq, CONV_HIST_ROWS, 2 * D_FF), _F32)],
        scratch_shapes=[pltpu.VMEM((CONV_HIST_ROWS, 2 * D_FF), _F32),
                        pltpu.VMEM((tile, D_FF), _BF16)],
        compiler_params=pltpu.CompilerParams(
            dimension_semantics=("arbitrary", "arbitrary"), vmem_limit_bytes=VMEM_LIMIT_BYTES),
        name="convffn",
    )(x, sf, *weights)


def _chunk_interleave(a):
    lead = a.shape[:-1]
    a = a.reshape(lead + (2, N_FFN_CHUNKS, FFN_CHUNK))
    return jnp.swapaxes(a, -3, -2).reshape(lead + (2 * D_FF,))


def _chunk_deinterleave(a):
    lead = a.shape[:-1]
    a = a.reshape(lead + (N_FFN_CHUNKS, 2, FFN_CHUNK))
    return jnp.swapaxes(a, -3, -2).reshape(lead + (2 * D_FF,))


def _pad_hist(state, rows):
    return jnp.pad(state, ((0, 0), (rows - state.shape[1], 0), (0, 0)))


def _run_stream(x, st_pool, st_conv, st_ffn, params, final_g, pos0, tile):
    pools, convs, ffns = [], [], []
    depth = len(params)
    for l, prm in enumerate(params):
        x, pool_new, conv_new = _mixer_call(
            x, _pad_hist(st_pool[l], POOL_HIST_ROWS), _pad_hist(st_conv[l], CONV_HIST_ROWS),
            prm, pos0, tile)
        x, ffn_new = _ffn_call(
            x, _pad_hist(_chunk_interleave(st_ffn[l]), CONV_HIST_ROWS), prm, final_g,
            l == depth - 1, tile)
        pools.append(pool_new[:, POOL_HIST_ROWS - POOL_HIST:])
        convs.append(conv_new[:, CONV_HIST_ROWS - (CONV_K - 1):])
        ffns.append(_chunk_deinterleave(ffn_new[:, CONV_HIST_ROWS - (CONV_K - 1):]))
    return x, jnp.stack(pools), jnp.stack(convs), jnp.stack(ffns)


def kernel(x_prompt, x_sample, state_pool, state_conv, state_ffn, norm_mix_g, w_in, b_gate,
           w_pool_map, pool_scale, conv_w, w_conv_out, w_o, norm_ffn_g, w_up, ffn_conv_w,
           ffn_conv_b, w_down, final_norm_g):
    depth = w_in.shape[0]
    params = []
    for l in range(depth):
        params.append(dict(
            g_mix=norm_mix_g[l].reshape(1, -1),
            w_in=w_in[l].astype(_BF16),
            b_gate=b_gate[l].reshape(1, -1),
            w_map=w_pool_map[l].astype(_BF16),
            pool_scale=pool_scale[l].reshape(1, -1),
            conv_w=conv_w[l],
            w_co=w_conv_out[l].astype(_BF16),
            w_o=w_o[l].astype(_BF16),
            g_ffn=norm_ffn_g[l].reshape(1, -1),
            w_up=_chunk_interleave(w_up[l]).astype(_BF16),
            ffn_w=_chunk_interleave(ffn_conv_w[l]),
            ffn_b=_chunk_interleave(ffn_conv_b[l]).reshape(1, -1),
            w_down=w_down[l].astype(_BF16),
        ))
    final_g = final_norm_g.reshape(1, -1)

    nb = x_prompt.shape[0]
    zeros = lambda r, c: [jnp.zeros((nb, r, c), _F32)] * depth
    yp, pp, cp, fp = _run_stream(
        x_prompt, zeros(POOL_HIST, POOL_WIDTH), zeros(CONV_K - 1, CONV_WIDTH),
        zeros(CONV_K - 1, 2 * D_FF), params, final_g, 0, PROMPT_TILE)
    ys, ps, cs, fs = _run_stream(
        x_sample, list(state_pool), list(state_conv), list(state_ffn), params, final_g,
        PAST_LEN, x_sample.shape[1])
    return yp, ys, pp, cp, fp, ps, cs, fs
```

```python
import functools

import jax
import jax.numpy as jnp
from jax import lax
from jax.experimental import pallas as pl
from jax.experimental.pallas import tpu as pltpu

D_MODEL = 1024
POOL_WINDOWS = (2, 4, 8, 16)
POOL_GROUP_DIM = 128
POOL_WIDTH = 512
POOL_HIST = 15
CONV_WIDTH = 512
CONV_TAPS = 3
CONV_HIST = CONV_TAPS - 1
D_FF = 2816
PAST_LEN = 1024
EPS = 1e-6

POOL_HIST_ROWS = 16
CONV_HIST_ROWS = 8
FFN_CHUNK = 256
N_FFN_CHUNKS = D_FF // FFN_CHUNK
PROMPT_TILE = 512
VMEM_LIMIT_BYTES = 56 * 1024 * 1024

_BF16 = jnp.bfloat16
_F32 = jnp.float32


def _dot(a, b):
    return jnp.dot(a, b, preferred_element_type=_F32)


def _rmsnorm(x, g):
    y = x * lax.rsqrt(jnp.mean(x * x, axis=-1, keepdims=True) + EPS)
    return y * g


def _rows(a, nseg, s):
    seg_len = a.shape[0] // nseg
    return a[s * seg_len:(s + 1) * seg_len]


def _cat_rows(parts):
    return parts[0] if len(parts) == 1 else jnp.concatenate(parts, axis=0)


def _causal_conv3(hist, u, w_ref, nseg):
    outs = []
    for s in range(nseg):
        us = _rows(u, nseg, s)
        ext = jnp.concatenate([hist[s], us], axis=0)
        prev2 = pltpu.roll(ext, 2, axis=0)[CONV_HIST_ROWS:]
        prev1 = pltpu.roll(ext, 1, axis=0)[CONV_HIST_ROWS:]
        outs.append(prev2 * w_ref[0:1, :] + prev1 * w_ref[1:2, :] + us * w_ref[2:3, :])
    return _cat_rows(outs)


def _init_hist(hist, state_ref, used):
    hist[...] = jnp.zeros(hist.shape, _F32)
    if state_ref is not None:
        hist[:, hist.shape[1] - used:, :] = state_ref[...]


def _mixer_kernel(pos0, has_state, *refs):
    if has_state:
        x_ref, sp_ref, sc_ref = refs[:3]
        refs = refs[3:]
    else:
        x_ref, sp_ref, sc_ref = refs[0], None, None
        refs = refs[1:]
    (g_ref, w_in_ref, bg_ref, wmap_ref, pscale_ref, convw_ref, wco_ref, wo_ref,
     out_ref, pool_out_ref, conv_out_ref, pool_hist, conv_hist) = refs
    t = pl.program_id(1)
    nseg, seg_len, _ = x_ref.shape
    rows = nseg * seg_len

    @pl.when(t == 0)
    def _():
        _init_hist(pool_hist, sp_ref, POOL_HIST)
        _init_hist(conv_hist, sc_ref, CONV_HIST)

    x = x_ref[...].reshape(rows, D_MODEL)
    hb = _rmsnorm(x, g_ref[...]).astype(_BF16)

    u = _dot(hb, w_in_ref[:, 0:POOL_WIDTH])
    pos = pos0 + t * seg_len + lax.broadcasted_iota(jnp.int32, (seg_len, POOL_GROUP_DIM), 0)
    ya_parts = []
    for gi, win in enumerate(POOL_WINDOWS):
        lanes = slice(gi * POOL_GROUP_DIM, (gi + 1) * POOL_GROUP_DIM)
        cnt = jnp.minimum(win, pos + 1).astype(_F32)
        mixed = []
        for s in range(nseg):
            us = _rows(u, nseg, s)[:, lanes]
            acc = jnp.concatenate([pool_hist[s, :, lanes], us], axis=0)
            k = 1
            while k < win:
                acc = acc + pltpu.roll(acc, k, axis=0)
                k *= 2
            mixed.append(acc[POOL_HIST_ROWS:] / cnt - us)
        ya_parts.append(_dot(_cat_rows(mixed).astype(_BF16), wmap_ref[gi]))
    y_a = jnp.concatenate(ya_parts, axis=1) * pscale_ref[...]
    for s in range(nseg):
        us = _rows(u, nseg, s)
        pool_hist[s] = us[seg_len - POOL_HIST_ROWS:]
        pool_out_ref[s] = us[seg_len - POOL_HIST:]

    o = POOL_WIDTH
    gb = _dot(hb, w_in_ref[:, o:o + CONV_WIDTH]); o += CONV_WIDTH
    gc = _dot(hb, w_in_ref[:, o:o + CONV_WIDTH]); o += CONV_WIDTH
    v = _dot(hb, w_in_ref[:, o:o + CONV_WIDTH]); o += CONV_WIDTH
    p = gc * v
    cv = _causal_conv3(conv_hist, p, convw_ref, nseg)
    for s in range(nseg):
        ps = _rows(p, nseg, s)
        conv_hist[s] = ps[seg_len - CONV_HIST_ROWS:]
        conv_out_ref[s] = ps[seg_len - CONV_HIST:]
    y_b = _dot((gb * cv).astype(_BF16), wco_ref[...])

    la = _dot(hb, w_in_ref[:, o:o + D_MODEL]) + bg_ref[:, 0:D_MODEL]
    merged = jax.nn.sigmoid(la) * y_a
    lb = _dot(hb, w_in_ref[:, o + D_MODEL:o + 2 * D_MODEL]) + bg_ref[:, D_MODEL:2 * D_MODEL]
    merged = merged + jax.nn.sigmoid(lb) * y_b
    y = x + _dot(merged.astype(_BF16), wo_ref[...])
    out_ref[...] = y.reshape(nseg, seg_len, D_MODEL)


def _ffn_kernel(final, has_state, *refs):
    if has_state:
        x_ref, sf_ref = refs[:2]
        refs = refs[2:]
    else:
        x_ref, sf_ref = refs[0], None
        refs = refs[1:]
    (g_ref, wup_ref, cw_ref, cb_ref, wdn_ref, fg_ref, out_ref, ffn_out_ref, hist, act_scr) = refs
    t = pl.program_id(1)
    nseg, seg_len, _ = x_ref.shape
    rows = nseg * seg_len

    @pl.when(t == 0)
    def _():
        _init_hist(hist, sf_ref, CONV_HIST)

    x = x_ref[...].reshape(rows, D_MODEL)
    hb = _rmsnorm(x, g_ref[...]).astype(_BF16)
    for c in range(N_FFN_CHUNKS):
        halves = []
        for base in (0, D_FF):
            cols = slice(base + c * FFN_CHUNK, base + (c + 1) * FFN_CHUNK)
            up = _dot(hb, wup_ref[:, cols])
            halves.append(_causal_conv3(hist.at[:, :, cols], up, cw_ref.at[:, cols], nseg)
                          + cb_ref[:, cols])
            for s in range(nseg):
                ups = _rows(up, nseg, s)
                hist[s, :, cols] = ups[seg_len - CONV_HIST_ROWS:]
                ffn_out_ref[s, :, cols] = ups[seg_len - CONV_HIST:]
        act = jax.nn.silu(halves[1]) * halves[0]
        act_scr[:, c * FFN_CHUNK:(c + 1) * FFN_CHUNK] = act.astype(_BF16)
    y = x + _dot(act_scr[...], wdn_ref[...])
    if final:
        y = _rmsnorm(y, fg_ref[...])
    out_ref[...] = y.reshape(nseg, seg_len, D_MODEL)


def _resident(shape):
    zeros = (0,) * len(shape)
    return pl.BlockSpec(shape, lambda b, t: zeros, pipeline_mode=pl.Buffered(1))


def _per_block(nseg, rows, width):
    return pl.BlockSpec((nseg, rows, width), lambda b, t: (b, 0, 0))


def _mixer_call(x, states, prm, pos0, nseg, seg_len):
    nseq, slen, _ = x.shape
    x_spec = pl.BlockSpec((nseg, seg_len, D_MODEL), lambda b, t: (b, t, 0))
    state_specs = [_per_block(nseg, POOL_HIST, POOL_WIDTH), _per_block(nseg, CONV_HIST, CONV_WIDTH)]
    weights = (prm["g_mix"], prm["w_in"], prm["b_gate"], prm["w_map"], prm["pool_scale"],
               prm["conv_w"], prm["w_co"], prm["w_o"])
    has_state = states is not None
    return pl.pallas_call(
        functools.partial(_mixer_kernel, pos0, has_state),
        grid=(nseq // nseg, slen // seg_len),
        in_specs=[x_spec] + (state_specs if has_state else []) + [_resident(w.shape) for w in weights],
        out_specs=[x_spec] + state_specs,
        out_shape=[jax.ShapeDtypeStruct(x.shape, _F32),
                   jax.ShapeDtypeStruct((nseq, POOL_HIST, POOL_WIDTH), _F32),
                   jax.ShapeDtypeStruct((nseq, CONV_HIST, CONV_WIDTH), _F32)],
        scratch_shapes=[pltpu.VMEM((nseg, POOL_HIST_ROWS, POOL_WIDTH), _F32),
                        pltpu.VMEM((nseg, CONV_HIST_ROWS, CONV_WIDTH), _F32)],
        compiler_params=pltpu.CompilerParams(
            dimension_semantics=("arbitrary", "arbitrary"), vmem_limit_bytes=VMEM_LIMIT_BYTES),
        name="mixer",
    )(x, *(states if has_state else ()), *weights)


def _ffn_call(x, state, prm, final_g, final, nseg, seg_len):
    nseq, slen, _ = x.shape
    x_spec = pl.BlockSpec((nseg, seg_len, D_MODEL), lambda b, t: (b, t, 0))
    state_spec = _per_block(nseg, CONV_HIST, 2 * D_FF)
    weights = (prm["g_ffn"], prm["w_up"], prm["ffn_w"], prm["ffn_b"], prm["w_down"], final_g)
    has_state = state is not None
    return pl.pallas_call(
        functools.partial(_ffn_kernel, final, has_state),
        grid=(nseq // nseg, slen // seg_len),
        in_specs=[x_spec] + ([state_spec] if has_state else []) + [_resident(w.shape) for w in weights],
        out_specs=[x_spec, state_spec],
        out_shape=[jax.ShapeDtypeStruct(x.shape, _F32),
                   jax.ShapeDtypeStruct((nseq, CONV_HIST, 2 * D_FF), _F32)],
        scratch_shapes=[pltpu.VMEM((nseg, CONV_HIST_ROWS, 2 * D_FF), _F32),
                        pltpu.VMEM((nseg * seg_len, D_FF), _BF16)],
        compiler_params=pltpu.CompilerParams(
            dimension_semantics=("arbitrary", "arbitrary"), vmem_limit_bytes=VMEM_LIMIT_BYTES),
        name="convffn",
    )(x, *((state,) if has_state else ()), *weights)


def _run_stream(x, states, params, final_g, pos0, nseg, seg_len):
    pools, convs, ffns = [], [], []
    depth = len(params)
    for l, prm in enumerate(params):
        st = None if states is None else (states[0][l], states[1][l])
        x, pool_new, conv_new = _mixer_call(x, st, prm, pos0, nseg, seg_len)
        st = None if states is None else states[2][l]
        x, ffn_new = _ffn_call(x, st, prm, final_g, l == depth - 1, nseg, seg_len)
        pools.append(pool_new)
        convs.append(conv_new)
        ffns.append(ffn_new)
    return x, jnp.stack(pools), jnp.stack(convs), jnp.stack(ffns)


def kernel(x_prompt, x_sample, state_pool, state_conv, state_ffn, norm_mix_g, w_in, b_gate,
           w_pool_map, pool_scale, conv_w, w_conv_out, w_o, norm_ffn_g, w_up, ffn_conv_w,
           ffn_conv_b, w_down, final_norm_g):
    depth = w_in.shape[0]
    params = []
    for l in range(depth):
        params.append(dict(
            g_mix=norm_mix_g[l].reshape(1, -1),
            w_in=w_in[l].astype(_BF16),
            b_gate=b_gate[l].reshape(1, -1),
            w_map=w_pool_map[l].astype(_BF16),
            pool_scale=pool_scale[l].reshape(1, -1),
            conv_w=conv_w[l],
            w_co=w_conv_out[l].astype(_BF16),
            w_o=w_o[l].astype(_BF16),
            g_ffn=norm_ffn_g[l].reshape(1, -1),
            w_up=w_up[l].astype(_BF16),
            ffn_w=ffn_conv_w[l],
            ffn_b=ffn_conv_b[l].reshape(1, -1),
            w_down=w_down[l].astype(_BF16),
        ))
    final_g = final_norm_g.reshape(1, -1)

    yp, pp, cp, fp = _run_stream(x_prompt, None, params, final_g, 0, 1, PROMPT_TILE)
    ys, ps, cs, fs = _run_stream(
        x_sample, (state_pool, state_conv, state_ffn), params, final_g, PAST_LEN,
        x_sample.shape[0], x_sample.shape[1])
    return yp, ys, pp, cp, fp, ps, cs, fs
```

```python
import functools

import jax
import jax.numpy as jnp
from jax import lax
from jax.experimental import pallas as pl
from jax.experimental.pallas import tpu as pltpu

D_MODEL = 1024
POOL_WINDOWS = (2, 4, 8, 16)
POOL_GROUP_DIM = 128
POOL_WIDTH = 512
POOL_HIST = 15
CONV_WIDTH = 512
CONV_TAPS = 3
CONV_HIST = CONV_TAPS - 1
D_FF = 2816
PAST_LEN = 1024
EPS = 1e-6

POOL_HIST_ROWS = 16
CONV_HIST_ROWS = 8
FFN_CHUNK = 256
N_FFN_CHUNKS = D_FF // FFN_CHUNK
PROMPT_TILE = 1024
VMEM_LIMIT_BYTES = 56 * 1024 * 1024

_BF16 = jnp.bfloat16
_F32 = jnp.float32


def _dot(a, b):
    return jnp.dot(a, b, preferred_element_type=_F32)


def _rmsnorm(x, g):
    y = x * lax.rsqrt(jnp.mean(x * x, axis=-1, keepdims=True) + EPS)
    return y * g


def _rows(a, nseg, s):
    seg_len = a.shape[0] // nseg
    return a[s * seg_len:(s + 1) * seg_len]


def _cat_rows(parts):
    return parts[0] if len(parts) == 1 else jnp.concatenate(parts, axis=0)


def _causal_conv3(hist, u, w_ref, nseg):
    outs = []
    for s in range(nseg):
        us = _rows(u, nseg, s)
        ext = jnp.concatenate([hist[s], us], axis=0)
        prev2 = pltpu.roll(ext, 2, axis=0)[CONV_HIST_ROWS:]
        prev1 = pltpu.roll(ext, 1, axis=0)[CONV_HIST_ROWS:]
        outs.append(prev2 * w_ref[0:1, :] + prev1 * w_ref[1:2, :] + us * w_ref[2:3, :])
    return _cat_rows(outs)


def _init_hist(hist, state_ref, used):
    hist[...] = jnp.zeros(hist.shape, _F32)
    if state_ref is not None:
        hist[:, hist.shape[1] - used:, :] = state_ref[...]


def _mixer_kernel(pos0, has_state, *refs):
    if has_state:
        x_ref, sp_ref, sc_ref = refs[:3]
        refs = refs[3:]
    else:
        x_ref, sp_ref, sc_ref = refs[0], None, None
        refs = refs[1:]
    (g_ref, w_in_ref, bg_ref, wmap_ref, pscale_ref, convw_ref, wco_ref, wo_ref,
     out_ref, pool_out_ref, conv_out_ref, pool_hist, conv_hist) = refs
    t = pl.program_id(1)
    nseg, seg_len, _ = x_ref.shape
    rows = nseg * seg_len

    @pl.when(t == 0)
    def _():
        _init_hist(pool_hist, sp_ref, POOL_HIST)
        _init_hist(conv_hist, sc_ref, CONV_HIST)

    x = x_ref[...].reshape(rows, D_MODEL)
    hb = _rmsnorm(x, g_ref[...]).astype(_BF16)

    u = _dot(hb, w_in_ref[:, 0:POOL_WIDTH])
    pos = pos0 + t * seg_len + lax.broadcasted_iota(jnp.int32, (seg_len, POOL_GROUP_DIM), 0)
    ya_parts = []
    for gi, win in enumerate(POOL_WINDOWS):
        lanes = slice(gi * POOL_GROUP_DIM, (gi + 1) * POOL_GROUP_DIM)
        cnt = jnp.minimum(win, pos + 1).astype(_F32)
        mixed = []
        for s in range(nseg):
            us = _rows(u, nseg, s)[:, lanes]
            acc = jnp.concatenate([pool_hist[s, :, lanes], us], axis=0)
            k = 1
            while k < win:
                acc = acc + pltpu.roll(acc, k, axis=0)
                k *= 2
            mixed.append(acc[POOL_HIST_ROWS:] / cnt - us)
        ya_parts.append(_dot(_cat_rows(mixed).astype(_BF16), wmap_ref[gi]))
    y_a = jnp.concatenate(ya_parts, axis=1) * pscale_ref[...]
    for s in range(nseg):
        us = _rows(u, nseg, s)
        pool_hist[s] = us[seg_len - POOL_HIST_ROWS:]
        pool_out_ref[s] = us[seg_len - POOL_HIST:]

    o = POOL_WIDTH
    gb = _dot(hb, w_in_ref[:, o:o + CONV_WIDTH]); o += CONV_WIDTH
    gc = _dot(hb, w_in_ref[:, o:o + CONV_WIDTH]); o += CONV_WIDTH
    v = _dot(hb, w_in_ref[:, o:o + CONV_WIDTH]); o += CONV_WIDTH
    p = gc * v
    cv = _causal_conv3(conv_hist, p, convw_ref, nseg)
    for s in range(nseg):
        ps = _rows(p, nseg, s)
        conv_hist[s] = ps[seg_len - CONV_HIST_ROWS:]
        conv_out_ref[s] = ps[seg_len - CONV_HIST:]
    y_b = _dot((gb * cv).astype(_BF16), wco_ref[...])

    la = _dot(hb, w_in_ref[:, o:o + D_MODEL]) + bg_ref[:, 0:D_MODEL]
    merged = jax.nn.sigmoid(la) * y_a
    lb = _dot(hb, w_in_ref[:, o + D_MODEL:o + 2 * D_MODEL]) + bg_ref[:, D_MODEL:2 * D_MODEL]
    merged = merged + jax.nn.sigmoid(lb) * y_b
    y = x + _dot(merged.astype(_BF16), wo_ref[...])
    out_ref[...] = y.reshape(nseg, seg_len, D_MODEL)


def _ffn_kernel(final, has_state, *refs):
    if has_state:
        x_ref, sf_ref = refs[:2]
        refs = refs[2:]
    else:
        x_ref, sf_ref = refs[0], None
        refs = refs[1:]
    (g_ref, wup_ref, cw_ref, cb_ref, wdn_ref, fg_ref, out_ref, ffn_out_ref, hist, act_scr) = refs
    t = pl.program_id(1)
    nseg, seg_len, _ = x_ref.shape
    rows = nseg * seg_len

    @pl.when(t == 0)
    def _():
        _init_hist(hist, sf_ref, CONV_HIST)

    x = x_ref[...].reshape(rows, D_MODEL)
    hb = _rmsnorm(x, g_ref[...]).astype(_BF16)
    for c in range(N_FFN_CHUNKS):
        halves = []
        for base in (0, D_FF):
            cols = slice(base + c * FFN_CHUNK, base + (c + 1) * FFN_CHUNK)
            up = _dot(hb, wup_ref[:, cols])
            halves.append(_causal_conv3(hist.at[:, :, cols], up, cw_ref.at[:, cols], nseg)
                          + cb_ref[:, cols])
            for s in range(nseg):
                ups = _rows(up, nseg, s)
                hist[s, :, cols] = ups[seg_len - CONV_HIST_ROWS:]
                ffn_out_ref[s, :, cols] = ups[seg_len - CONV_HIST:]
        act = jax.nn.silu(halves[1]) * halves[0]
        act_scr[:, c * FFN_CHUNK:(c + 1) * FFN_CHUNK] = act.astype(_BF16)
    y = x + _dot(act_scr[...], wdn_ref[...])
    if final:
        y = _rmsnorm(y, fg_ref[...])
    out_ref[...] = y.reshape(nseg, seg_len, D_MODEL)


def _resident(shape):
    zeros = (0,) * len(shape)
    return pl.BlockSpec(shape, lambda b, t: zeros, pipeline_mode=pl.Buffered(1))


def _per_block(nseg, rows, width):
    return pl.BlockSpec((nseg, rows, width), lambda b, t: (b, 0, 0))


def _mixer_call(x, states, prm, pos0, nseg, seg_len):
    nseq, slen, _ = x.shape
    x_spec = pl.BlockSpec((nseg, seg_len, D_MODEL), lambda b, t: (b, t, 0))
    state_specs = [_per_block(nseg, POOL_HIST, POOL_WIDTH), _per_block(nseg, CONV_HIST, CONV_WIDTH)]
    weights = (prm["g_mix"], prm["w_in"], prm["b_gate"], prm["w_map"], prm["pool_scale"],
               prm["conv_w"], prm["w_co"], prm["w_o"])
    has_state = states is not None
    return pl.pallas_call(
        functools.partial(_mixer_kernel, pos0, has_state),
        grid=(nseq // nseg, slen // seg_len),
        in_specs=[x_spec] + (state_specs if has_state else []) + [_resident(w.shape) for w in weights],
        out_specs=[x_spec] + state_specs,
        out_shape=[jax.ShapeDtypeStruct(x.shape, _F32),
                   jax.ShapeDtypeStruct((nseq, POOL_HIST, POOL_WIDTH), _F32),
                   jax.ShapeDtypeStruct((nseq, CONV_HIST, CONV_WIDTH), _F32)],
        scratch_shapes=[pltpu.VMEM((nseg, POOL_HIST_ROWS, POOL_WIDTH), _F32),
                        pltpu.VMEM((nseg, CONV_HIST_ROWS, CONV_WIDTH), _F32)],
        compiler_params=pltpu.CompilerParams(
            dimension_semantics=("arbitrary", "arbitrary"), vmem_limit_bytes=VMEM_LIMIT_BYTES),
        name="mixer",
    )(x, *(states if has_state else ()), *weights)


def _ffn_call(x, state, prm, final_g, final, nseg, seg_len):
    nseq, slen, _ = x.shape
    x_spec = pl.BlockSpec((nseg, seg_len, D_MODEL), lambda b, t: (b, t, 0))
    state_spec = _per_block(nseg, CONV_HIST, 2 * D_FF)
    weights = (prm["g_ffn"], prm["w_up"], prm["ffn_w"], prm["ffn_b"], prm["w_down"], final_g)
    has_state = state is not None
    return pl.pallas_call(
        functools.partial(_ffn_kernel, final, has_state),
        grid=(nseq // nseg, slen // seg_len),
        in_specs=[x_spec] + ([state_spec] if has_state else []) + [_resident(w.shape) for w in weights],
        out_specs=[x_spec, state_spec],
        out_shape=[jax.ShapeDtypeStruct(x.shape, _F32),
                   jax.ShapeDtypeStruct((nseq, CONV_HIST, 2 * D_FF), _F32)],
        scratch_shapes=[pltpu.VMEM((nseg, CONV_HIST_ROWS, 2 * D_FF), _F32),
                        pltpu.VMEM((nseg * seg_len, D_FF), _BF16)],
        compiler_params=pltpu.CompilerParams(
            dimension_semantics=("arbitrary", "arbitrary"), vmem_limit_bytes=VMEM_LIMIT_BYTES),
        name="convffn",
    )(x, *((state,) if has_state else ()), *weights)


def _run_stream(x, states, params, final_g, pos0, nseg, seg_len):
    pools, convs, ffns = [], [], []
    depth = len(params)
    for l, prm in enumerate(params):
        st = None if states is None else (states[0][l], states[1][l])
        x, pool_new, conv_new = _mixer_call(x, st, prm, pos0, nseg, seg_len)
        st = None if states is None else states[2][l]
        x, ffn_new = _ffn_call(x, st, prm, final_g, l == depth - 1, nseg, seg_len)
        pools.append(pool_new)
        convs.append(conv_new)
        ffns.append(ffn_new)
    return x, jnp.stack(pools), jnp.stack(convs), jnp.stack(ffns)


def kernel(x_prompt, x_sample, state_pool, state_conv, state_ffn, norm_mix_g, w_in, b_gate,
           w_pool_map, pool_scale, conv_w, w_conv_out, w_o, norm_ffn_g, w_up, ffn_conv_w,
           ffn_conv_b, w_down, final_norm_g):
    depth = w_in.shape[0]
    params = []
    for l in range(depth):
        params.append(dict(
            g_mix=norm_mix_g[l].reshape(1, -1),
            w_in=w_in[l].astype(_BF16),
            b_gate=b_gate[l].reshape(1, -1),
            w_map=w_pool_map[l].astype(_BF16),
            pool_scale=pool_scale[l].reshape(1, -1),
            conv_w=conv_w[l],
            w_co=w_conv_out[l].astype(_BF16),
            w_o=w_o[l].astype(_BF16),
            g_ffn=norm_ffn_g[l].reshape(1, -1),
            w_up=w_up[l].astype(_BF16),
            ffn_w=ffn_conv_w[l],
            ffn_b=ffn_conv_b[l].reshape(1, -1),
            w_down=w_down[l].astype(_BF16),
        ))
    final_g = final_norm_g.reshape(1, -1)

    yp, pp, cp, fp = _run_stream(x_prompt, None, params, final_g, 0, 1, PROMPT_TILE)
    ys, ps, cs, fs = _run_stream(
        x_sample, (state_pool, state_conv, state_ffn), params, final_g, PAST_LEN,
        x_sample.shape[0], x_sample.shape[1])
    return yp, ys, pp, cp, fp, ps, cs, fs
```

```python
import functools

import jax
import jax.numpy as jnp
from jax import lax
from jax.experimental import pallas as pl
from jax.experimental.pallas import tpu as pltpu

D_MODEL = 1024
POOL_WINDOWS = (2, 4, 8, 16)
POOL_GROUP_DIM = 128
POOL_OUT_GROUP = 256
POOL_WIDTH = 512
POOL_HIST = 15
CONV_WIDTH = 512
CONV_TAPS = 3
CONV_HIST = CONV_TAPS - 1
D_FF = 2816
PAST_LEN = 1024
EPS = 1e-6

POOL_HIST_ROWS = 16
CONV_HIST_ROWS = 8
FFN_CHUNK = 256
N_FFN_CHUNKS = D_FF // FFN_CHUNK
PROMPT_TILE = 1024
VMEM_LIMIT_BYTES = 56 * 1024 * 1024

_BF16 = jnp.bfloat16
_F32 = jnp.float32


def _dot(a, b):
    return jnp.dot(a, b, preferred_element_type=_F32)


def _rmsnorm(x, g):
    y = x * lax.rsqrt(jnp.mean(x * x, axis=-1, keepdims=True) + EPS)
    return y * g


def _rows(a, nseg, s):
    seg_len = a.shape[0] // nseg
    return a[s * seg_len:(s + 1) * seg_len]


def _cat_rows(parts):
    return parts[0] if len(parts) == 1 else jnp.concatenate(parts, axis=0)


def _causal_conv3(hist, u, w_ref, nseg):
    outs = []
    for s in range(nseg):
        us = _rows(u, nseg, s)
        ext = jnp.concatenate([hist[s], us], axis=0)
        prev2 = pltpu.roll(ext, 2, axis=0)[CONV_HIST_ROWS:]
        prev1 = pltpu.roll(ext, 1, axis=0)[CONV_HIST_ROWS:]
        outs.append(prev2 * w_ref[0:1, :] + prev1 * w_ref[1:2, :] + us * w_ref[2:3, :])
    return _cat_rows(outs)


def _init_hist(hist, state_ref, used):
    hist[...] = jnp.zeros(hist.shape, _F32)
    if state_ref is not None:
        hist[:, hist.shape[1] - used:, :] = state_ref[...]


def _mixer_kernel(pos0, has_state, *refs):
    x_ref, refs = refs[0], refs[1:]
    sp_ref = sc_ref = None
    if has_state:
        sp_ref, sc_ref, refs = refs[0], refs[1], refs[2:]
    (g_ref, w_in_ref, bg_ref, wmap_ref, pscale_ref, convw_ref, wco_ref, wo_ref,
     out_ref, pool_out_ref, conv_out_ref, pool_hist, conv_hist, merged_scr) = refs
    t = pl.program_id(1)
    nseg, seg_len, _ = x_ref.shape
    rows = nseg * seg_len

    @pl.when(t == 0)
    def _():
        _init_hist(pool_hist, sp_ref, POOL_HIST)
        _init_hist(conv_hist, sc_ref, CONV_HIST)

    x = x_ref[...].reshape(rows, D_MODEL)
    hb = _rmsnorm(x, g_ref[...]).astype(_BF16)

    o_gb = POOL_WIDTH
    o_gc = o_gb + CONV_WIDTH
    o_v = o_gc + CONV_WIDTH
    o_gate = o_v + CONV_WIDTH
    u = _dot(hb, w_in_ref[:, 0:POOL_WIDTH])
    gc = _dot(hb, w_in_ref[:, o_gc:o_gc + CONV_WIDTH])
    v = _dot(hb, w_in_ref[:, o_v:o_v + CONV_WIDTH])
    for s in range(nseg):
        pool_out_ref[s] = _rows(u, nseg, s)[seg_len - POOL_HIST:]

    p = gc * v
    cv = _causal_conv3(conv_hist, p, convw_ref, nseg)
    for s in range(nseg):
        ps = _rows(p, nseg, s)
        conv_hist[s] = ps[seg_len - CONV_HIST_ROWS:]
        conv_out_ref[s] = ps[seg_len - CONV_HIST:]
    gb = _dot(hb, w_in_ref[:, o_gb:o_gb + CONV_WIDTH])
    q = (gb * cv).astype(_BF16)

    pos = pos0 + t * seg_len + lax.broadcasted_iota(jnp.int32, (seg_len, POOL_GROUP_DIM), 0)
    for gi, win in enumerate(POOL_WINDOWS):
        lanes = slice(gi * POOL_GROUP_DIM, (gi + 1) * POOL_GROUP_DIM)
        cols = slice(gi * POOL_OUT_GROUP, (gi + 1) * POOL_OUT_GROUP)
        cols_b = slice(D_MODEL + gi * POOL_OUT_GROUP, D_MODEL + (gi + 1) * POOL_OUT_GROUP)
        la = _dot(hb, w_in_ref[:, o_gate + cols.start:o_gate + cols.stop]) + bg_ref[:, cols]
        lb = _dot(hb, w_in_ref[:, o_gate + cols_b.start:o_gate + cols_b.stop]) + bg_ref[:, cols_b]
        y_b = _dot(q, wco_ref[:, cols])
        cnt = jnp.minimum(win, pos + 1).astype(_F32)
        mixed = []
        for s in range(nseg):
            us = _rows(u, nseg, s)[:, lanes]
            acc = jnp.concatenate([pool_hist[s, :, lanes], us], axis=0)
            k = 1
            while k < win:
                acc = acc + pltpu.roll(acc, k, axis=0)
                k *= 2
            mixed.append(acc[POOL_HIST_ROWS:] / cnt - us)
        y_a = _dot(_cat_rows(mixed).astype(_BF16), wmap_ref[gi]) * pscale_ref[:, cols]
        merged = jax.nn.sigmoid(la) * y_a + jax.nn.sigmoid(lb) * y_b
        merged_scr[:, cols] = merged.astype(_BF16)
    for s in range(nseg):
        pool_hist[s] = _rows(u, nseg, s)[seg_len - POOL_HIST_ROWS:]

    y = x + _dot(merged_scr[...], wo_ref[...])
    out_ref[...] = y.reshape(nseg, seg_len, D_MODEL)


def _ffn_kernel(final, has_state, *refs):
    x_ref, refs = refs[0], refs[1:]
    sf_ref = None
    if has_state:
        sf_ref, refs = refs[0], refs[1:]
    (g_ref, wup_ref, cw_ref, cb_ref, wdn_ref, fg_ref, out_ref, ffn_out_ref, hist, act_scr) = refs
    t = pl.program_id(1)
    nseg, seg_len, _ = x_ref.shape
    rows = nseg * seg_len

    @pl.when(t == 0)
    def _():
        _init_hist(hist, sf_ref, CONV_HIST)

    x = x_ref[...].reshape(rows, D_MODEL)
    hb = _rmsnorm(x, g_ref[...]).astype(_BF16)
    for c in range(N_FFN_CHUNKS):
        halves = []
        for base in (0, D_FF):
            cols = slice(base + c * FFN_CHUNK, base + (c + 1) * FFN_CHUNK)
            up = _dot(hb, wup_ref[:, cols])
            halves.append(_causal_conv3(hist.at[:, :, cols], up, cw_ref.at[:, cols], nseg)
                          + cb_ref[:, cols])
            for s in range(nseg):
                ups = _rows(up, nseg, s)
                hist[s, :, cols] = ups[seg_len - CONV_HIST_ROWS:]
                ffn_out_ref[s, :, cols] = ups[seg_len - CONV_HIST:]
        act = jax.nn.silu(halves[1]) * halves[0]
        act_scr[:, c * FFN_CHUNK:(c + 1) * FFN_CHUNK] = act.astype(_BF16)
    y = x + _dot(act_scr[...], wdn_ref[...])
    if final:
        y = _rmsnorm(y, fg_ref[...])
    out_ref[...] = y.reshape(nseg, seg_len, D_MODEL)


def _resident(shape):
    zeros = (0,) * len(shape)
    return pl.BlockSpec(shape, lambda b, t: zeros, pipeline_mode=pl.Buffered(1))


def _per_block(nseg, rows, width):
    return pl.BlockSpec((nseg, rows, width), lambda b, t: (b, 0, 0))


def _mixer_call(x, states, prm, pos0, nseg, seg_len):
    nseq, slen, _ = x.shape
    x_spec = pl.BlockSpec((nseg, seg_len, D_MODEL), lambda b, t: (b, t, 0))
    state_specs = [_per_block(nseg, POOL_HIST, POOL_WIDTH), _per_block(nseg, CONV_HIST, CONV_WIDTH)]
    weights = (prm["g_mix"], prm["w_in"], prm["b_gate"], prm["w_map"], prm["pool_scale"],
               prm["conv_w"], prm["w_co"], prm["w_o"])
    has_state = states is not None
    return pl.pallas_call(
        functools.partial(_mixer_kernel, pos0, has_state),
        grid=(nseq // nseg, slen // seg_len),
        in_specs=[x_spec] + (state_specs if has_state else []) + [_resident(w.shape) for w in weights],
        out_specs=[x_spec] + state_specs,
        out_shape=[jax.ShapeDtypeStruct(x.shape, _F32),
                   jax.ShapeDtypeStruct((nseq, POOL_HIST, POOL_WIDTH), _F32),
                   jax.ShapeDtypeStruct((nseq, CONV_HIST, CONV_WIDTH), _F32)],
        scratch_shapes=[pltpu.VMEM((nseg, POOL_HIST_ROWS, POOL_WIDTH), _F32),
                        pltpu.VMEM((nseg, CONV_HIST_ROWS, CONV_WIDTH), _F32),
                        pltpu.VMEM((nseg * seg_len, D_MODEL), _BF16)],
        compiler_params=pltpu.CompilerParams(
            dimension_semantics=("arbitrary", "arbitrary"), vmem_limit_bytes=VMEM_LIMIT_BYTES),
        name="mixer",
    )(x, *(states if has_state else ()), *weights)


def _ffn_call(x, state, prm, final_g, final, nseg, seg_len):
    nseq, slen, _ = x.shape
    x_spec = pl.BlockSpec((nseg, seg_len, D_MODEL), lambda b, t: (b, t, 0))
    state_spec = _per_block(nseg, CONV_HIST, 2 * D_FF)
    weights = (prm["g_ffn"], prm["w_up"], prm["ffn_w"], prm["ffn_b"], prm["w_down"], final_g)
    has_state = state is not None
    return pl.pallas_call(
        functools.partial(_ffn_kernel, final, has_state),
        grid=(nseq // nseg, slen // seg_len),
        in_specs=[x_spec] + ([state_spec] if has_state else []) + [_resident(w.shape) for w in weights],
        out_specs=[x_spec, state_spec],
        out_shape=[jax.ShapeDtypeStruct(x.shape, _F32),
                   jax.ShapeDtypeStruct((nseq, CONV_HIST, 2 * D_FF), _F32)],
        scratch_shapes=[pltpu.VMEM((nseg, CONV_HIST_ROWS, 2 * D_FF), _F32),
                        pltpu.VMEM((nseg * seg_len, D_FF), _BF16)],
        compiler_params=pltpu.CompilerParams(
            dimension_semantics=("arbitrary", "arbitrary"), vmem_limit_bytes=VMEM_LIMIT_BYTES),
        name="convffn",
    )(x, *((state,) if has_state else ()), *weights)


def _run_stream(x, states, params, final_g, pos0, nseg, seg_len):
    pools, convs, ffns = [], [], []
    depth = len(params)
    for l, prm in enumerate(params):
        st = None if states is None else (states[0][l], states[1][l])
        x, pool_new, conv_new = _mixer_call(x, st, prm, pos0, nseg, seg_len)
        st = None if states is None else states[2][l]
        x, ffn_new = _ffn_call(x, st, prm, final_g, l == depth - 1, nseg, seg_len)
        pools.append(pool_new)
        convs.append(conv_new)
        ffns.append(ffn_new)
    return x, jnp.stack(pools), jnp.stack(convs), jnp.stack(ffns)


def kernel(x_prompt, x_sample, state_pool, state_conv, state_ffn, norm_mix_g, w_in, b_gate,
           w_pool_map, pool_scale, conv_w, w_conv_out, w_o, norm_ffn_g, w_up, ffn_conv_w,
           ffn_conv_b, w_down, final_norm_g):
    depth = w_in.shape[0]
    params = []
    for l in range(depth):
        params.append(dict(
            g_mix=norm_mix_g[l].reshape(1, -1),
            w_in=w_in[l].astype(_BF16),
            b_gate=b_gate[l].reshape(1, -1),
            w_map=w_pool_map[l].astype(_BF16),
            pool_scale=pool_scale[l].reshape(1, -1),
            conv_w=conv_w[l],
            w_co=w_conv_out[l].astype(_BF16),
            w_o=w_o[l].astype(_BF16),
            g_ffn=norm_ffn_g[l].reshape(1, -1),
            w_up=w_up[l].astype(_BF16),
            ffn_w=ffn_conv_w[l],
            ffn_b=ffn_conv_b[l].reshape(1, -1),
            w_down=w_down[l].astype(_BF16),
        ))
    final_g = final_norm_g.reshape(1, -1)

    yp, pp, cp, fp = _run_stream(x_prompt, None, params, final_g, 0, 1, PROMPT_TILE)
    ys, ps, cs, fs = _run_stream(
        x_sample, (state_pool, state_conv, state_ffn), params, final_g, PAST_LEN,
        x_sample.shape[0], x_sample.shape[1])
    return yp, ys, pp, cp, fp, ps, cs, fs
```

```python
import functools

import jax
import jax.numpy as jnp
from jax import lax
from jax.experimental import pallas as pl
from jax.experimental.pallas import tpu as pltpu

D_MODEL = 1024
POOL_WINDOWS = (2, 4, 8, 16)
POOL_GROUP_DIM = 128
POOL_OUT_GROUP = 256
POOL_WIDTH = 512
POOL_HIST = 15
CONV_WIDTH = 512
CONV_TAPS = 3
CONV_HIST = CONV_TAPS - 1
D_FF = 2816
PAST_LEN = 1024
EPS = 1e-6

POOL_HIST_ROWS = 16
CONV_HIST_ROWS = 8
LANES = 128
RESIDUES = 8
TAIL_ROWS = 8
OUT_CHUNK = 256
FFN_CHUNK = 256
N_FFN_CHUNKS = D_FF // FFN_CHUNK
PROMPT_TILE = 1024
VMEM_LIMIT_BYTES = 56 * 1024 * 1024

_BF16 = jnp.bfloat16
_F32 = jnp.float32


def _dot(a, b):
    return jnp.dot(a, b, preferred_element_type=_F32)


def _rmsnorm(x, g):
    y = x * lax.rsqrt(jnp.mean(x * x, axis=-1, keepdims=True) + EPS)
    return y * g


def _rows(a, nseg, s):
    seg_len = a.shape[0] // nseg
    return a[s * seg_len:(s + 1) * seg_len]


def _cat_rows(parts):
    return parts[0] if len(parts) == 1 else jnp.concatenate(parts, axis=0)


def _causal_conv3(hist, u, w_ref, nseg):
    outs = []
    for s in range(nseg):
        us = _rows(u, nseg, s)
        ext = jnp.concatenate([hist[s], us], axis=0)
        prev2 = pltpu.roll(ext, 2, axis=0)[CONV_HIST_ROWS:]
        prev1 = pltpu.roll(ext, 1, axis=0)[CONV_HIST_ROWS:]
        outs.append(prev2 * w_ref[0:1, :] + prev1 * w_ref[1:2, :] + us * w_ref[2:3, :])
    return _cat_rows(outs)


def _init_hist(hist, state_ref, used):
    hist[...] = jnp.zeros(hist.shape, _F32)
    if state_ref is not None:
        hist[:, hist.shape[1] - used:, :] = state_ref[...]


def _mixer_kernel(pos0, has_state, *refs):
    x_ref, refs = refs[0], refs[1:]
    sp_ref = sc_ref = None
    if has_state:
        sp_ref, sc_ref, refs = refs[0], refs[1], refs[2:]
    (g_ref, w_in_ref, bg_ref, wmap_ref, pscale_ref, convw_ref, wco_ref, wo_ref,
     out_ref, pool_out_ref, conv_out_ref, pool_hist, conv_hist, merged_scr) = refs
    t = pl.program_id(1)
    nseg, seg_len, _ = x_ref.shape
    rows = nseg * seg_len

    @pl.when(t == 0)
    def _():
        _init_hist(pool_hist, sp_ref, POOL_HIST)
        _init_hist(conv_hist, sc_ref, CONV_HIST)

    x = x_ref[...].reshape(rows, D_MODEL)
    hb = _rmsnorm(x, g_ref[...]).astype(_BF16)

    o_gb = POOL_WIDTH
    o_gc = o_gb + CONV_WIDTH
    o_v = o_gc + CONV_WIDTH
    o_gate = o_v + CONV_WIDTH
    u = _dot(hb, w_in_ref[:, 0:POOL_WIDTH])
    gc = _dot(hb, w_in_ref[:, o_gc:o_gc + CONV_WIDTH])
    v = _dot(hb, w_in_ref[:, o_v:o_v + CONV_WIDTH])
    for s in range(nseg):
        pool_out_ref[s] = _rows(u, nseg, s)[seg_len - POOL_HIST:]

    p = gc * v
    cv = _causal_conv3(conv_hist, p, convw_ref, nseg)
    for s in range(nseg):
        ps = _rows(p, nseg, s)
        conv_hist[s] = ps[seg_len - CONV_HIST_ROWS:]
        conv_out_ref[s] = ps[seg_len - CONV_HIST:]
    gb = _dot(hb, w_in_ref[:, o_gb:o_gb + CONV_WIDTH])
    q = (gb * cv).astype(_BF16)

    pos = pos0 + t * seg_len + lax.broadcasted_iota(jnp.int32, (seg_len, POOL_GROUP_DIM), 0)
    for gi, win in enumerate(POOL_WINDOWS):
        lanes = slice(gi * POOL_GROUP_DIM, (gi + 1) * POOL_GROUP_DIM)
        cols = slice(gi * POOL_OUT_GROUP, (gi + 1) * POOL_OUT_GROUP)
        cols_b = slice(D_MODEL + gi * POOL_OUT_GROUP, D_MODEL + (gi + 1) * POOL_OUT_GROUP)
        la = _dot(hb, w_in_ref[:, o_gate + cols.start:o_gate + cols.stop]) + bg_ref[:, cols]
        lb = _dot(hb, w_in_ref[:, o_gate + cols_b.start:o_gate + cols_b.stop]) + bg_ref[:, cols_b]
        y_b = _dot(q, wco_ref[:, cols])
        cnt = jnp.minimum(win, pos + 1).astype(_F32)
        mixed = []
        for s in range(nseg):
            us = _rows(u, nseg, s)[:, lanes]
            acc = jnp.concatenate([pool_hist[s, :, lanes], us], axis=0)
            k = 1
            while k < win:
                acc = acc + pltpu.roll(acc, k, axis=0)
                k *= 2
            mixed.append(acc[POOL_HIST_ROWS:] / cnt - us)
        y_a = _dot(_cat_rows(mixed).astype(_BF16), wmap_ref[gi]) * pscale_ref[:, cols]
        merged = jax.nn.sigmoid(la) * y_a + jax.nn.sigmoid(lb) * y_b
        merged_scr[:, cols] = merged.astype(_BF16)
    for s in range(nseg):
        pool_hist[s] = _rows(u, nseg, s)[seg_len - POOL_HIST_ROWS:]

    y = x + _dot(merged_scr[...], wo_ref[...])
    out_ref[...] = y.reshape(nseg, seg_len, D_MODEL)


def _to_residue_major(perm, a, nseg):
    seg_len = a.shape[0] // nseg
    bl = seg_len // RESIDUES
    for c, slab in enumerate(perm):
        slab[...] = a[:, c * LANES:(c + 1) * LANES]
    blocks = []
    for s in range(nseg):
        for r in range(RESIDUES):
            blocks.append(jnp.concatenate(
                [slab[pl.ds(s * seg_len + r, bl, stride=RESIDUES), :] for slab in perm], axis=1))
    return jnp.concatenate(blocks, axis=0)


def _to_time_major(perm, a, nseg):
    seg_len = a.shape[0] // nseg
    bl = seg_len // RESIDUES
    for s in range(nseg):
        for r in range(RESIDUES):
            blk = a[s * seg_len + r * bl:s * seg_len + (r + 1) * bl]
            for c, slab in enumerate(perm):
                slab[pl.ds(s * seg_len + r, bl, stride=RESIDUES), :] = blk[:, c * LANES:(c + 1) * LANES]
    return jnp.concatenate([slab[...] for slab in perm], axis=1)


def _delay_block(tail, blk):
    ext = jnp.concatenate([tail, blk], axis=0)
    return pltpu.roll(ext, 1, axis=0)[TAIL_ROWS:]


def _causal_conv3_residue(tails, u, w_ref, nseg):
    outs = []
    for s in range(nseg):
        us = _rows(u, nseg, s)
        bl = us.shape[0] // RESIDUES
        d6 = _delay_block(tails[s, 0], us[6 * bl:7 * bl])
        d7 = _delay_block(tails[s, 1], us[7 * bl:8 * bl])
        prev1 = jnp.concatenate([d7, us[:7 * bl]], axis=0)
        prev2 = jnp.concatenate([d6, d7, us[:6 * bl]], axis=0)
        outs.append(prev2 * w_ref[0:1, :] + prev1 * w_ref[1:2, :] + us * w_ref[2:3, :])
    return _cat_rows(outs)


def _ffn_kernel(final, has_state, *refs):
    x_ref, refs = refs[0], refs[1:]
    sf_ref = None
    if has_state:
        sf_ref, refs = refs[0], refs[1:]
    (g_ref, wup_ref, cw_ref, cb_ref, wdn_ref, fg_ref, out_ref, ffn_out_ref, tails,
     act_scr) = refs[:10]
    perm = refs[10:]
    t = pl.program_id(1)
    nseg, seg_len, _ = x_ref.shape
    rows = nseg * seg_len
    bl = seg_len // RESIDUES

    @pl.when(t == 0)
    def _():
        tails[...] = jnp.zeros(tails.shape, _F32)
        if sf_ref is not None:
            for k in range(CONV_HIST):
                tails[:, k, TAIL_ROWS - 1:TAIL_ROWS, :] = sf_ref[:, k:k + 1, :]

    x = x_ref[...].reshape(rows, D_MODEL)
    hb = _to_residue_major(perm, _rmsnorm(x, g_ref[...]), nseg).astype(_BF16)
    for c in range(N_FFN_CHUNKS):
        halves = []
        for base in (0, D_FF):
            cols = slice(base + c * FFN_CHUNK, base + (c + 1) * FFN_CHUNK)
            up = _dot(hb, wup_ref[:, cols])
            halves.append(
                _causal_conv3_residue(tails.at[:, :, :, cols], up, cw_ref.at[:, cols], nseg)
                + cb_ref[:, cols])
            for s in range(nseg):
                ups = _rows(up, nseg, s)
                for k in range(CONV_HIST):
                    end = (RESIDUES - CONV_HIST + k + 1) * bl
                    tails[s, k, :, cols] = ups[end - TAIL_ROWS:end]
                    ffn_out_ref[s, k:k + 1, cols] = ups[end - 1:end]
        act = jax.nn.silu(halves[1]) * halves[0]
        act_scr[:, c * FFN_CHUNK:(c + 1) * FFN_CHUNK] = act.astype(_BF16)
    slabs_per_chunk = OUT_CHUNK // LANES
    parts = []
    for j in range(D_MODEL // OUT_CHUNK):
        cols = slice(j * OUT_CHUNK, (j + 1) * OUT_CHUNK)
        slabs = perm[j * slabs_per_chunk:(j + 1) * slabs_per_chunk]
        part = x[:, cols] + _to_time_major(slabs, _dot(act_scr[...], wdn_ref[:, cols]), nseg)
        if final:
            parts.append(part)
        else:
            out_ref[:, :, cols] = part.reshape(nseg, seg_len, OUT_CHUNK)
    if final:
        y = _rmsnorm(jnp.concatenate(parts, axis=1), fg_ref[...])
        out_ref[...] = y.reshape(nseg, seg_len, D_MODEL)


def _resident(shape):
    zeros = (0,) * len(shape)
    return pl.BlockSpec(shape, lambda b, t: zeros, pipeline_mode=pl.Buffered(1))


def _per_block(nseg, rows, width):
    return pl.BlockSpec((nseg, rows, width), lambda b, t: (b, 0, 0))


def _mixer_call(x, states, prm, pos0, nseg, seg_len):
    nseq, slen, _ = x.shape
    x_spec = pl.BlockSpec((nseg, seg_len, D_MODEL), lambda b, t: (b, t, 0))
    state_specs = [_per_block(nseg, POOL_HIST, POOL_WIDTH), _per_block(nseg, CONV_HIST, CONV_WIDTH)]
    weights = (prm["g_mix"], prm["w_in"], prm["b_gate"], prm["w_map"], prm["pool_scale"],
               prm["conv_w"], prm["w_co"], prm["w_o"])
    has_state = states is not None
    return pl.pallas_call(
        functools.partial(_mixer_kernel, pos0, has_state),
        grid=(nseq // nseg, slen // seg_len),
        in_specs=[x_spec] + (state_specs if has_state else []) + [_resident(w.shape) for w in weights],
        out_specs=[x_spec] + state_specs,
        out_shape=[jax.ShapeDtypeStruct(x.shape, _F32),
                   jax.ShapeDtypeStruct((nseq, POOL_HIST, POOL_WIDTH), _F32),
                   jax.ShapeDtypeStruct((nseq, CONV_HIST, CONV_WIDTH), _F32)],
        scratch_shapes=[pltpu.VMEM((nseg, POOL_HIST_ROWS, POOL_WIDTH), _F32),
                        pltpu.VMEM((nseg, CONV_HIST_ROWS, CONV_WIDTH), _F32),
                        pltpu.VMEM((nseg * seg_len, D_MODEL), _BF16)],
        compiler_params=pltpu.CompilerParams(
            dimension_semantics=("arbitrary", "arbitrary"), vmem_limit_bytes=VMEM_LIMIT_BYTES),
        name="mixer",
    )(x, *(states if has_state else ()), *weights)


def _ffn_call(x, state, prm, final_g, final, nseg, seg_len):
    nseq, slen, _ = x.shape
    x_spec = pl.BlockSpec((nseg, seg_len, D_MODEL), lambda b, t: (b, t, 0))
    state_spec = _per_block(nseg, CONV_HIST, 2 * D_FF)
    weights = (prm["g_ffn"], prm["w_up"], prm["ffn_w"], prm["ffn_b"], prm["w_down"], final_g)
    has_state = state is not None
    return pl.pallas_call(
        functools.partial(_ffn_kernel, final, has_state),
        grid=(nseq // nseg, slen // seg_len),
        in_specs=[x_spec] + ([state_spec] if has_state else []) + [_resident(w.shape) for w in weights],
        out_specs=[x_spec, state_spec],
        out_shape=[jax.ShapeDtypeStruct(x.shape, _F32),
                   jax.ShapeDtypeStruct((nseq, CONV_HIST, 2 * D_FF), _F32)],
        scratch_shapes=[pltpu.VMEM((nseg, CONV_HIST, TAIL_ROWS, 2 * D_FF), _F32),
                        pltpu.VMEM((nseg * seg_len, D_FF), _BF16)]
                       + [pltpu.VMEM((nseg * seg_len, LANES), _F32)] * (D_MODEL // LANES),
        compiler_params=pltpu.CompilerParams(
            dimension_semantics=("arbitrary", "arbitrary"), vmem_limit_bytes=VMEM_LIMIT_BYTES),
        name="convffn",
    )(x, *((state,) if has_state else ()), *weights)


def _run_stream(x, states, params, final_g, pos0, nseg, seg_len):
    pools, convs, ffns = [], [], []
    depth = len(params)
    for l, prm in enumerate(params):
        st = None if states is None else (states[0][l], states[1][l])
        x, pool_new, conv_new = _mixer_call(x, st, prm, pos0, nseg, seg_len)
        st = None if states is None else states[2][l]
        x, ffn_new = _ffn_call(x, st, prm, final_g, l == depth - 1, nseg, seg_len)
        pools.append(pool_new)
        convs.append(conv_new)
        ffns.append(ffn_new)
    return x, jnp.stack(pools), jnp.stack(convs), jnp.stack(ffns)


def kernel(x_prompt, x_sample, state_pool, state_conv, state_ffn, norm_mix_g, w_in, b_gate,
           w_pool_map, pool_scale, conv_w, w_conv_out, w_o, norm_ffn_g, w_up, ffn_conv_w,
           ffn_conv_b, w_down, final_norm_g):
    depth = w_in.shape[0]
    params = []
    for l in range(depth):
        params.append(dict(
            g_mix=norm_mix_g[l].reshape(1, -1),
            w_in=w_in[l].astype(_BF16),
            b_gate=b_gate[l].reshape(1, -1),
            w_map=w_pool_map[l].astype(_BF16),
            pool_scale=pool_scale[l].reshape(1, -1),
            conv_w=conv_w[l],
            w_co=w_conv_out[l].astype(_BF16),
            w_o=w_o[l].astype(_BF16),
            g_ffn=norm_ffn_g[l].reshape(1, -1),
            w_up=w_up[l].astype(_BF16),
            ffn_w=ffn_conv_w[l],
            ffn_b=ffn_conv_b[l].reshape(1, -1),
            w_down=w_down[l].astype(_BF16),
        ))
    final_g = final_norm_g.reshape(1, -1)

    yp, pp, cp, fp = _run_stream(x_prompt, None, params, final_g, 0, 1, PROMPT_TILE)
    ys, ps, cs, fs = _run_stream(
        x_sample, (state_pool, state_conv, state_ffn), params, final_g, PAST_LEN,
        x_sample.shape[0], x_sample.shape[1])
    return yp, ys, pp, cp, fp, ps, cs, fs
```

```python
import functools

import jax
import jax.numpy as jnp
from jax import lax
from jax.experimental import pallas as pl
from jax.experimental.pallas import tpu as pltpu

D_MODEL = 1024
POOL_WINDOWS = (2, 4, 8, 16)
POOL_GROUP_DIM = 128
POOL_OUT_GROUP = 256
POOL_WIDTH = 512
POOL_HIST = 15
CONV_WIDTH = 512
CONV_TAPS = 3
CONV_HIST = CONV_TAPS - 1
D_FF = 2816
PAST_LEN = 1024
EPS = 1e-6

POOL_HIST_ROWS = 16
CONV_HIST_ROWS = 8
LANES = 128
RESIDUES = 8
TAIL_ROWS = 8
FFN_CHUNK = 256
N_FFN_CHUNKS = D_FF // FFN_CHUNK
PROMPT_TILE = 1024
VMEM_LIMIT_BYTES = 56 * 1024 * 1024

_BF16 = jnp.bfloat16
_F32 = jnp.float32


def _dot(a, b):
    return jnp.dot(a, b, preferred_element_type=_F32)


def _rmsnorm(x, g):
    y = x * lax.rsqrt(jnp.mean(x * x, axis=-1, keepdims=True) + EPS)
    return y * g


def _rows(a, nseg, s):
    seg_len = a.shape[0] // nseg
    return a[s * seg_len:(s + 1) * seg_len]


def _cat_rows(parts):
    return parts[0] if len(parts) == 1 else jnp.concatenate(parts, axis=0)


def _causal_conv3(hist, u, w_ref, nseg):
    outs = []
    for s in range(nseg):
        us = _rows(u, nseg, s)
        ext = jnp.concatenate([hist[s], us], axis=0)
        prev2 = pltpu.roll(ext, 2, axis=0)[CONV_HIST_ROWS:]
        prev1 = pltpu.roll(ext, 1, axis=0)[CONV_HIST_ROWS:]
        outs.append(prev2 * w_ref[0:1, :] + prev1 * w_ref[1:2, :] + us * w_ref[2:3, :])
    return _cat_rows(outs)


def _init_hist(hist, state_ref, used):
    hist[...] = jnp.zeros(hist.shape, _F32)
    if state_ref is not None:
        hist[:, hist.shape[1] - used:, :] = state_ref[...]


def _mixer_kernel(pos0, has_state, *refs):
    x_ref, refs = refs[0], refs[1:]
    sp_ref = sc_ref = None
    if has_state:
        sp_ref, sc_ref, refs = refs[0], refs[1], refs[2:]
    (g_ref, w_in_ref, bg_ref, wmap_ref, pscale_ref, convw_ref, wco_ref, wo_ref,
     out_ref, pool_out_ref, conv_out_ref, pool_hist, conv_hist, merged_scr) = refs
    t = pl.program_id(1)
    nseg, seg_len, _ = x_ref.shape
    rows = nseg * seg_len

    @pl.when(t == 0)
    def _():
        _init_hist(pool_hist, sp_ref, POOL_HIST)
        _init_hist(conv_hist, sc_ref, CONV_HIST)

    x = x_ref[...].reshape(rows, D_MODEL)
    hb = _rmsnorm(x, g_ref[...]).astype(_BF16)

    o_gb = POOL_WIDTH
    o_gc = o_gb + CONV_WIDTH
    o_v = o_gc + CONV_WIDTH
    o_gate = o_v + CONV_WIDTH
    u = _dot(hb, w_in_ref[:, 0:POOL_WIDTH])
    gc = _dot(hb, w_in_ref[:, o_gc:o_gc + CONV_WIDTH])
    v = _dot(hb, w_in_ref[:, o_v:o_v + CONV_WIDTH])
    for s in range(nseg):
        pool_out_ref[s] = _rows(u, nseg, s)[seg_len - POOL_HIST:]

    p = gc * v
    cv = _causal_conv3(conv_hist, p, convw_ref, nseg)
    for s in range(nseg):
        ps = _rows(p, nseg, s)
        conv_hist[s] = ps[seg_len - CONV_HIST_ROWS:]
        conv_out_ref[s] = ps[seg_len - CONV_HIST:]
    gb = _dot(hb, w_in_ref[:, o_gb:o_gb + CONV_WIDTH])
    q = (gb * cv).astype(_BF16)

    pos = pos0 + t * seg_len + lax.broadcasted_iota(jnp.int32, (seg_len, POOL_GROUP_DIM), 0)
    for gi, win in enumerate(POOL_WINDOWS):
        lanes = slice(gi * POOL_GROUP_DIM, (gi + 1) * POOL_GROUP_DIM)
        cols = slice(gi * POOL_OUT_GROUP, (gi + 1) * POOL_OUT_GROUP)
        cols_b = slice(D_MODEL + gi * POOL_OUT_GROUP, D_MODEL + (gi + 1) * POOL_OUT_GROUP)
        la = _dot(hb, w_in_ref[:, o_gate + cols.start:o_gate + cols.stop]) + bg_ref[:, cols]
        lb = _dot(hb, w_in_ref[:, o_gate + cols_b.start:o_gate + cols_b.stop]) + bg_ref[:, cols_b]
        y_b = _dot(q, wco_ref[:, cols])
        cnt = jnp.minimum(win, pos + 1).astype(_F32)
        mixed = []
        for s in range(nseg):
            us = _rows(u, nseg, s)[:, lanes]
            acc = jnp.concatenate([pool_hist[s, :, lanes], us], axis=0)
            k = 1
            while k < win:
                acc = acc + pltpu.roll(acc, k, axis=0)
                k *= 2
            mixed.append(acc[POOL_HIST_ROWS:] / cnt - us)
        y_a = _dot(_cat_rows(mixed).astype(_BF16), wmap_ref[gi]) * pscale_ref[:, cols]
        merged = jax.nn.sigmoid(la) * y_a + jax.nn.sigmoid(lb) * y_b
        merged_scr[:, cols] = merged.astype(_BF16)
    for s in range(nseg):
        pool_hist[s] = _rows(u, nseg, s)[seg_len - POOL_HIST_ROWS:]

    y = x + _dot(merged_scr[...], wo_ref[...])
    out_ref[...] = y.reshape(nseg, seg_len, D_MODEL)


def _to_residue_major(perm, a, nseg):
    seg_len = a.shape[0] // nseg
    bl = seg_len // RESIDUES
    for c, slab in enumerate(perm):
        slab[...] = a[:, c * LANES:(c + 1) * LANES]
    blocks = []
    for s in range(nseg):
        for r in range(RESIDUES):
            blocks.append(jnp.concatenate(
                [slab[pl.ds(s * seg_len + r, bl, stride=RESIDUES), :] for slab in perm], axis=1))
    return jnp.concatenate(blocks, axis=0)


def _to_time_major(perm, a, nseg):
    seg_len = a.shape[0] // nseg
    bl = seg_len // RESIDUES
    for s in range(nseg):
        for r in range(RESIDUES):
            blk = a[s * seg_len + r * bl:s * seg_len + (r + 1) * bl]
            for c, slab in enumerate(perm):
                slab[pl.ds(s * seg_len + r, bl, stride=RESIDUES), :] = blk[:, c * LANES:(c + 1) * LANES]
    return jnp.concatenate([slab[...] for slab in perm], axis=1)


def _delay_block(tail, blk):
    ext = jnp.concatenate([tail, blk], axis=0)
    return pltpu.roll(ext, 1, axis=0)[TAIL_ROWS:]


def _causal_conv3_residue(tails, u, w_ref, nseg):
    outs = []
    for s in range(nseg):
        us = _rows(u, nseg, s)
        bl = us.shape[0] // RESIDUES
        d6 = _delay_block(tails[s, 0], us[6 * bl:7 * bl])
        d7 = _delay_block(tails[s, 1], us[7 * bl:8 * bl])
        prev1 = jnp.concatenate([d7, us[:7 * bl]], axis=0)
        prev2 = jnp.concatenate([d6, d7, us[:6 * bl]], axis=0)
        outs.append(prev2 * w_ref[0:1, :] + prev1 * w_ref[1:2, :] + us * w_ref[2:3, :])
    return _cat_rows(outs)


def _ffn_kernel(final, has_state, *refs):
    x_ref, refs = refs[0], refs[1:]
    sf_ref = None
    if has_state:
        sf_ref, refs = refs[0], refs[1:]
    (g_ref, wup_ref, cw_ref, cb_ref, wdn_ref, fg_ref, out_ref, ffn_out_ref, tails,
     act_scr) = refs[:10]
    perm = refs[10:]
    t = pl.program_id(1)
    nseg, seg_len, _ = x_ref.shape
    rows = nseg * seg_len
    bl = seg_len // RESIDUES

    @pl.when(t == 0)
    def _():
        tails[...] = jnp.zeros(tails.shape, _F32)
        if sf_ref is not None:
            for k in range(CONV_HIST):
                tails[:, k, TAIL_ROWS - 1:TAIL_ROWS, :] = sf_ref[:, k:k + 1, :]

    x = x_ref[...].reshape(rows, D_MODEL)
    hb = _to_residue_major(perm, _rmsnorm(x, g_ref[...]), nseg).astype(_BF16)
    for c in range(N_FFN_CHUNKS):
        halves = []
        for base in (0, D_FF):
            cols = slice(base + c * FFN_CHUNK, base + (c + 1) * FFN_CHUNK)
            up = _dot(hb, wup_ref[:, cols])
            halves.append(
                _causal_conv3_residue(tails.at[:, :, :, cols], up, cw_ref.at[:, cols], nseg)
                + cb_ref[:, cols])
            for s in range(nseg):
                ups = _rows(up, nseg, s)
                for k in range(CONV_HIST):
                    end = (RESIDUES - CONV_HIST + k + 1) * bl
                    tails[s, k, :, cols] = ups[end - TAIL_ROWS:end]
                    ffn_out_ref[s, k:k + 1, cols] = ups[end - 1:end]
        act = jax.nn.silu(halves[1]) * halves[0]
        act_scr[:, c * FFN_CHUNK:(c + 1) * FFN_CHUNK] = act.astype(_BF16)
    y = x + _to_time_major(perm, _dot(act_scr[...], wdn_ref[...]), nseg)
    if final:
        y = _rmsnorm(y, fg_ref[...])
    out_ref[...] = y.reshape(nseg, seg_len, D_MODEL)


def _resident(shape):
    zeros = (0,) * len(shape)
    return pl.BlockSpec(shape, lambda b, t: zeros, pipeline_mode=pl.Buffered(1))


def _per_block(nseg, rows, width):
    return pl.BlockSpec((nseg, rows, width), lambda b, t: (b, 0, 0))


def _mixer_call(x, states, prm, pos0, nseg, seg_len):
    nseq, slen, _ = x.shape
    x_spec = pl.BlockSpec((nseg, seg_len, D_MODEL), lambda b, t: (b, t, 0))
    state_specs = [_per_block(nseg, POOL_HIST, POOL_WIDTH), _per_block(nseg, CONV_HIST, CONV_WIDTH)]
    weights = (prm["g_mix"], prm["w_in"], prm["b_gate"], prm["w_map"], prm["pool_scale"],
               prm["conv_w"], prm["w_co"], prm["w_o"])
    has_state = states is not None
    return pl.pallas_call(
        functools.partial(_mixer_kernel, pos0, has_state),
        grid=(nseq // nseg, slen // seg_len),
        in_specs=[x_spec] + (state_specs if has_state else []) + [_resident(w.shape) for w in weights],
        out_specs=[x_spec] + state_specs,
        out_shape=[jax.ShapeDtypeStruct(x.shape, _F32),
                   jax.ShapeDtypeStruct((nseq, POOL_HIST, POOL_WIDTH), _F32),
                   jax.ShapeDtypeStruct((nseq, CONV_HIST, CONV_WIDTH), _F32)],
        scratch_shapes=[pltpu.VMEM((nseg, POOL_HIST_ROWS, POOL_WIDTH), _F32),
                        pltpu.VMEM((nseg, CONV_HIST_ROWS, CONV_WIDTH), _F32),
                        pltpu.VMEM((nseg * seg_len, D_MODEL), _BF16)],
        compiler_params=pltpu.CompilerParams(
            dimension_semantics=("arbitrary", "arbitrary"), vmem_limit_bytes=VMEM_LIMIT_BYTES),
        name="mixer",
    )(x, *(states if has_state else ()), *weights)


def _ffn_call(x, state, prm, final_g, final, nseg, seg_len):
    nseq, slen, _ = x.shape
    x_spec = pl.BlockSpec((nseg, seg_len, D_MODEL), lambda b, t: (b, t, 0))
    state_spec = _per_block(nseg, CONV_HIST, 2 * D_FF)
    weights = (prm["g_ffn"], prm["w_up"], prm["ffn_w"], prm["ffn_b"], prm["w_down"], final_g)
    has_state = state is not None
    return pl.pallas_call(
        functools.partial(_ffn_kernel, final, has_state),
        grid=(nseq // nseg, slen // seg_len),
        in_specs=[x_spec] + ([state_spec] if has_state else []) + [_resident(w.shape) for w in weights],
        out_specs=[x_spec, state_spec],
        out_shape=[jax.ShapeDtypeStruct(x.shape, _F32),
                   jax.ShapeDtypeStruct((nseq, CONV_HIST, 2 * D_FF), _F32)],
        scratch_shapes=[pltpu.VMEM((nseg, CONV_HIST, TAIL_ROWS, 2 * D_FF), _F32),
                        pltpu.VMEM((nseg * seg_len, D_FF), _BF16)]
                       + [pltpu.VMEM((nseg * seg_len, LANES), _F32)] * (D_MODEL // LANES),
        compiler_params=pltpu.CompilerParams(
            dimension_semantics=("arbitrary", "arbitrary"), vmem_limit_bytes=VMEM_LIMIT_BYTES),
        name="convffn",
    )(x, *((state,) if has_state else ()), *weights)


def _run_stream(x, states, params, final_g, pos0, nseg, seg_len):
    pools, convs, ffns = [], [], []
    depth = len(params)
    for l, prm in enumerate(params):
        st = None if states is None else (states[0][l], states[1][l])
        x, pool_new, conv_new = _mixer_call(x, st, prm, pos0, nseg, seg_len)
        st = None if states is None else states[2][l]
        x, ffn_new = _ffn_call(x, st, prm, final_g, l == depth - 1, nseg, seg_len)
        pools.append(pool_new)
        convs.append(conv_new)
        ffns.append(ffn_new)
    return x, jnp.stack(pools), jnp.stack(convs), jnp.stack(ffns)


def kernel(x_prompt, x_sample, state_pool, state_conv, state_ffn, norm_mix_g, w_in, b_gate,
           w_pool_map, pool_scale, conv_w, w_conv_out, w_o, norm_ffn_g, w_up, ffn_conv_w,
           ffn_conv_b, w_down, final_norm_g):
    depth = w_in.shape[0]
    params = []
    for l in range(depth):
        params.append(dict(
            g_mix=norm_mix_g[l].reshape(1, -1),
            w_in=w_in[l].astype(_BF16),
            b_gate=b_gate[l].reshape(1, -1),
            w_map=w_pool_map[l].astype(_BF16),
            pool_scale=pool_scale[l].reshape(1, -1),
            conv_w=conv_w[l],
            w_co=w_conv_out[l].astype(_BF16),
            w_o=w_o[l].astype(_BF16),
            g_ffn=norm_ffn_g[l].reshape(1, -1),
            w_up=w_up[l].astype(_BF16),
            ffn_w=ffn_conv_w[l],
            ffn_b=ffn_conv_b[l].reshape(1, -1),
            w_down=w_down[l].astype(_BF16),
        ))
    final_g = final_norm_g.reshape(1, -1)

    yp, pp, cp, fp = _run_stream(x_prompt, None, params, final_g, 0, 1, PROMPT_TILE)
    ys, ps, cs, fs = _run_stream(
        x_sample, (state_pool, state_conv, state_ffn), params, final_g, PAST_LEN,
        x_sample.shape[0], x_sample.shape[1])
    return yp, ys, pp, cp, fp, ps, cs, fs
```

```python
import functools

import jax
import jax.numpy as jnp
from jax import lax
from jax.experimental import pallas as pl
from jax.experimental.pallas import tpu as pltpu

D_MODEL = 1024
POOL_WINDOWS = (2, 4, 8, 16)
POOL_GROUP_DIM = 128
POOL_OUT_GROUP = 256
POOL_WIDTH = 512
POOL_HIST = 15
CONV_WIDTH = 512
CONV_TAPS = 3
CONV_HIST = CONV_TAPS - 1
D_FF = 2816
PAST_LEN = 1024
EPS = 1e-6

POOL_HIST_ROWS = 16
CONV_HIST_ROWS = 8
LANES = 128
RESIDUES = 8
TAIL_ROWS = 8
FFN_CHUNK = 256
N_FFN_CHUNKS = D_FF // FFN_CHUNK
PROMPT_TILE = 1024
VMEM_LIMIT_BYTES = 56 * 1024 * 1024

_BF16 = jnp.bfloat16
_F32 = jnp.float32


def _dot(a, b):
    return jnp.dot(a, b, preferred_element_type=_F32)


def _rmsnorm(x, g):
    y = x * lax.rsqrt(jnp.mean(x * x, axis=-1, keepdims=True) + EPS)
    return y * g


def _rows(a, nseg, s):
    seg_len = a.shape[0] // nseg
    return a[s * seg_len:(s + 1) * seg_len]


def _cat_rows(parts):
    return parts[0] if len(parts) == 1 else jnp.concatenate(parts, axis=0)


def _causal_conv3(hist, u, w_ref, nseg):
    outs = []
    for s in range(nseg):
        us = _rows(u, nseg, s)
        ext = jnp.concatenate([hist[s], us], axis=0)
        prev2 = pltpu.roll(ext, 2, axis=0)[CONV_HIST_ROWS:]
        prev1 = pltpu.roll(ext, 1, axis=0)[CONV_HIST_ROWS:]
        outs.append(prev2 * w_ref[0:1, :] + prev1 * w_ref[1:2, :] + us * w_ref[2:3, :])
    return _cat_rows(outs)


def _init_hist(hist, state_ref, used):
    hist[...] = jnp.zeros(hist.shape, _F32)
    if state_ref is not None:
        hist[:, hist.shape[1] - used:, :] = state_ref[...]


def _mixer_kernel(pos0, has_state, *refs):
    x_ref, refs = refs[0], refs[1:]
    sp_ref = sc_ref = None
    if has_state:
        sp_ref, sc_ref, refs = refs[0], refs[1], refs[2:]
    (g_ref, w_in_ref, bg_ref, wmap_ref, pscale_ref, convw_ref, wco_ref, wo_ref,
     out_ref, pool_out_ref, conv_out_ref, pool_hist, conv_hist, merged_scr) = refs
    t = pl.program_id(1)
    nseg, seg_len, _ = x_ref.shape
    rows = nseg * seg_len

    @pl.when(t == 0)
    def _():
        _init_hist(pool_hist, sp_ref, POOL_HIST)
        _init_hist(conv_hist, sc_ref, CONV_HIST)

    x = x_ref[...].reshape(rows, D_MODEL)
    hb = _rmsnorm(x, g_ref[...]).astype(_BF16)

    o_gb = POOL_WIDTH
    o_gc = o_gb + CONV_WIDTH
    o_v = o_gc + CONV_WIDTH
    o_gate = o_v + CONV_WIDTH
    u = _dot(hb, w_in_ref[:, 0:POOL_WIDTH])
    gc = _dot(hb, w_in_ref[:, o_gc:o_gc + CONV_WIDTH])
    v = _dot(hb, w_in_ref[:, o_v:o_v + CONV_WIDTH])
    for s in range(nseg):
        pool_out_ref[s] = _rows(u, nseg, s)[seg_len - POOL_HIST:]

    p = gc * v
    cv = _causal_conv3(conv_hist, p, convw_ref, nseg)
    for s in range(nseg):
        ps = _rows(p, nseg, s)
        conv_hist[s] = ps[seg_len - CONV_HIST_ROWS:]
        conv_out_ref[s] = ps[seg_len - CONV_HIST:]
    gb = _dot(hb, w_in_ref[:, o_gb:o_gb + CONV_WIDTH])
    q = (gb * cv).astype(_BF16)

    pos = pos0 + t * seg_len + lax.broadcasted_iota(jnp.int32, (seg_len, POOL_GROUP_DIM), 0)
    for gi, win in enumerate(POOL_WINDOWS):
        lanes = slice(gi * POOL_GROUP_DIM, (gi + 1) * POOL_GROUP_DIM)
        cols = slice(gi * POOL_OUT_GROUP, (gi + 1) * POOL_OUT_GROUP)
        cols_b = slice(D_MODEL + gi * POOL_OUT_GROUP, D_MODEL + (gi + 1) * POOL_OUT_GROUP)
        la = _dot(hb, w_in_ref[:, o_gate + cols.start:o_gate + cols.stop]) + bg_ref[:, cols]
        lb = _dot(hb, w_in_ref[:, o_gate + cols_b.start:o_gate + cols_b.stop]) + bg_ref[:, cols_b]
        y_b = _dot(q, wco_ref[:, cols])
        cnt = jnp.minimum(win, pos + 1).astype(_F32)
        mixed = []
        for s in range(nseg):
            us = _rows(u, nseg, s)[:, lanes]
            acc = jnp.concatenate([pool_hist[s, :, lanes], us], axis=0)
            k = 1
            while k < win:
                acc = acc + pltpu.roll(acc, k, axis=0)
                k *= 2
            mixed.append(acc[POOL_HIST_ROWS:] / cnt - us)
        y_a = _dot(_cat_rows(mixed).astype(_BF16), wmap_ref[gi]) * pscale_ref[:, cols]
        merged = jax.nn.sigmoid(la) * y_a + jax.nn.sigmoid(lb) * y_b
        merged_scr[:, cols] = merged.astype(_BF16)
    for s in range(nseg):
        pool_hist[s] = _rows(u, nseg, s)[seg_len - POOL_HIST_ROWS:]

    y = x + _dot(merged_scr[...], wo_ref[...])
    out_ref[...] = y.reshape(nseg, seg_len, D_MODEL)


def _to_residue_major(perm, a, nseg):
    seg_len = a.shape[0] // nseg
    bl = seg_len // RESIDUES
    for c, slab in enumerate(perm):
        slab[...] = a[:, c * LANES:(c + 1) * LANES]
    blocks = []
    for s in range(nseg):
        for r in range(RESIDUES):
            blocks.append(jnp.concatenate(
                [slab[pl.ds(s * seg_len + r, bl, stride=RESIDUES), :] for slab in perm], axis=1))
    return jnp.concatenate(blocks, axis=0)


def _to_time_major(perm, a, nseg):
    seg_len = a.shape[0] // nseg
    bl = seg_len // RESIDUES
    for s in range(nseg):
        for r in range(RESIDUES):
            blk = a[s * seg_len + r * bl:s * seg_len + (r + 1) * bl]
            for c, slab in enumerate(perm):
                slab[pl.ds(s * seg_len + r, bl, stride=RESIDUES), :] = blk[:, c * LANES:(c + 1) * LANES]
    return jnp.concatenate([slab[...] for slab in perm], axis=1)


def _delay_block(tail, blk):
    ext = jnp.concatenate([tail, blk], axis=0)
    return pltpu.roll(ext, 1, axis=0)[TAIL_ROWS:]


def _causal_conv3_residue(tails, u, w_ref, nseg):
    outs = []
    for s in range(nseg):
        us = _rows(u, nseg, s)
        bl = us.shape[0] // RESIDUES
        d6 = _delay_block(tails[s, 0], us[6 * bl:7 * bl])
        d7 = _delay_block(tails[s, 1], us[7 * bl:8 * bl])
        prev1 = jnp.concatenate([d7, us[:7 * bl]], axis=0)
        prev2 = jnp.concatenate([d6, d7, us[:6 * bl]], axis=0)
        outs.append(prev2 * w_ref[0:1, :] + prev1 * w_ref[1:2, :] + us * w_ref[2:3, :])
    return _cat_rows(outs)


def _ffn_kernel(final, has_state, *refs):
    x_ref, refs = refs[0], refs[1:]
    sf_ref = None
    if has_state:
        sf_ref, refs = refs[0], refs[1:]
    (g_ref, wup_ref, cw_ref, cb_ref, wdn_ref, fg_ref, out_ref, ffn_out_ref, tails,
     act_scr) = refs[:10]
    perm = refs[10:]
    t = pl.program_id(1)
    nseg, seg_len, _ = x_ref.shape
    rows = nseg * seg_len
    bl = seg_len // RESIDUES

    @pl.when(t == 0)
    def _():
        tails[...] = jnp.zeros(tails.shape, _F32)
        if sf_ref is not None:
            for k in range(CONV_HIST):
                tails[:, k, TAIL_ROWS - 1:TAIL_ROWS, :] = sf_ref[:, k:k + 1, :]

    x = x_ref[...].reshape(rows, D_MODEL)
    hb = _to_residue_major(perm, _rmsnorm(x, g_ref[...]), nseg).astype(_BF16)
    for c in range(N_FFN_CHUNKS):
        halves = []
        for base in (0, D_FF):
            cols = slice(base + c * FFN_CHUNK, base + (c + 1) * FFN_CHUNK)
            up = _dot(hb, wup_ref[:, cols])
            halves.append(
                _causal_conv3_residue(tails.at[:, :, :, cols], up, cw_ref.at[:, cols], nseg)
                + cb_ref[:, cols])
            for s in range(nseg):
                ups = _rows(up, nseg, s)
                for k in range(CONV_HIST):
                    end = (RESIDUES - CONV_HIST + k + 1) * bl
                    tails[s, k, :, cols] = ups[end - TAIL_ROWS:end]
                    ffn_out_ref[s, k:k + 1, cols] = ups[end - 1:end]
        act = jax.nn.silu(halves[1]) * halves[0]
        act_scr[:, c * FFN_CHUNK:(c + 1) * FFN_CHUNK] = act.astype(_BF16)
    y = x + _to_time_major(perm, _dot(act_scr[...], wdn_ref[...]), nseg)
    if final:
        y = _rmsnorm(y, fg_ref[...])
    out_ref[...] = y.reshape(nseg, seg_len, D_MODEL)


def _resident(shape, layer):
    index = (layer,) + (0,) * (len(shape) - 1)
    return pl.BlockSpec((None,) + tuple(shape[1:]), lambda b, t: index,
                        pipeline_mode=pl.Buffered(1))


def _per_block(nseg, rows, width):
    return pl.BlockSpec((nseg, rows, width), lambda b, t: (b, 0, 0))


def _mixer_call(x, states, prm, layer, pos0, nseg, seg_len):
    nseq, slen, _ = x.shape
    x_spec = pl.BlockSpec((nseg, seg_len, D_MODEL), lambda b, t: (b, t, 0))
    state_specs = [_per_block(nseg, POOL_HIST, POOL_WIDTH), _per_block(nseg, CONV_HIST, CONV_WIDTH)]
    weights = (prm["g_mix"], prm["w_in"], prm["b_gate"], prm["w_map"], prm["pool_scale"],
               prm["conv_w"], prm["w_co"], prm["w_o"])
    has_state = states is not None
    return pl.pallas_call(
        functools.partial(_mixer_kernel, pos0, has_state),
        grid=(nseq // nseg, slen // seg_len),
        in_specs=([x_spec] + (state_specs if has_state else [])
                  + [_resident(w.shape, layer) for w in weights]),
        out_specs=[x_spec] + state_specs,
        out_shape=[jax.ShapeDtypeStruct(x.shape, _F32),
                   jax.ShapeDtypeStruct((nseq, POOL_HIST, POOL_WIDTH), _F32),
                   jax.ShapeDtypeStruct((nseq, CONV_HIST, CONV_WIDTH), _F32)],
        scratch_shapes=[pltpu.VMEM((nseg, POOL_HIST_ROWS, POOL_WIDTH), _F32),
                        pltpu.VMEM((nseg, CONV_HIST_ROWS, CONV_WIDTH), _F32),
                        pltpu.VMEM((nseg * seg_len, D_MODEL), _BF16)],
        compiler_params=pltpu.CompilerParams(
            dimension_semantics=("arbitrary", "arbitrary"), vmem_limit_bytes=VMEM_LIMIT_BYTES),
        name="mixer",
    )(x, *(states if has_state else ()), *weights)


def _ffn_call(x, state, prm, layer, final, nseg, seg_len):
    nseq, slen, _ = x.shape
    x_spec = pl.BlockSpec((nseg, seg_len, D_MODEL), lambda b, t: (b, t, 0))
    state_spec = _per_block(nseg, CONV_HIST, 2 * D_FF)
    weights = (prm["g_ffn"], prm["w_up"], prm["ffn_w"], prm["ffn_b"], prm["w_down"])
    final_g = prm["final_g"]
    has_state = state is not None
    return pl.pallas_call(
        functools.partial(_ffn_kernel, final, has_state),
        grid=(nseq // nseg, slen // seg_len),
        in_specs=([x_spec] + ([state_spec] if has_state else [])
                  + [_resident(w.shape, layer) for w in weights] + [_resident(final_g.shape, 0)]),
        out_specs=[x_spec, state_spec],
        out_shape=[jax.ShapeDtypeStruct(x.shape, _F32),
                   jax.ShapeDtypeStruct((nseq, CONV_HIST, 2 * D_FF), _F32)],
        scratch_shapes=[pltpu.VMEM((nseg, CONV_HIST, TAIL_ROWS, 2 * D_FF), _F32),
                        pltpu.VMEM((nseg * seg_len, D_FF), _BF16)]
                       + [pltpu.VMEM((nseg * seg_len, LANES), _F32)] * (D_MODEL // LANES),
        compiler_params=pltpu.CompilerParams(
            dimension_semantics=("arbitrary", "arbitrary"), vmem_limit_bytes=VMEM_LIMIT_BYTES),
        name="convffn",
    )(x, *((state,) if has_state else ()), *weights, final_g)


def _run_stream(x, states, prm, depth, pos0, nseg, seg_len):
    pools, convs, ffns = [], [], []
    for l in range(depth):
        st = None if states is None else (states[0][l], states[1][l])
        x, pool_new, conv_new = _mixer_call(x, st, prm, l, pos0, nseg, seg_len)
        st = None if states is None else states[2][l]
        x, ffn_new = _ffn_call(x, st, prm, l, l == depth - 1, nseg, seg_len)
        pools.append(pool_new)
        convs.append(conv_new)
        ffns.append(ffn_new)
    return x, jnp.stack(pools), jnp.stack(convs), jnp.stack(ffns)


def kernel(x_prompt, x_sample, state_pool, state_conv, state_ffn, norm_mix_g, w_in, b_gate,
           w_pool_map, pool_scale, conv_w, w_conv_out, w_o, norm_ffn_g, w_up, ffn_conv_w,
           ffn_conv_b, w_down, final_norm_g):
    depth = w_in.shape[0]
    row = lambda a: a.reshape(a.shape[0], 1, a.shape[-1])
    prm = dict(
        g_mix=row(norm_mix_g), w_in=w_in.astype(_BF16), b_gate=row(b_gate),
        w_map=w_pool_map.astype(_BF16), pool_scale=row(pool_scale), conv_w=conv_w,
        w_co=w_conv_out.astype(_BF16), w_o=w_o.astype(_BF16),
        g_ffn=row(norm_ffn_g), w_up=w_up.astype(_BF16), ffn_w=ffn_conv_w, ffn_b=row(ffn_conv_b),
        w_down=w_down.astype(_BF16), final_g=final_norm_g.reshape(1, 1, -1))

    yp, pp, cp, fp = _run_stream(x_prompt, None, prm, depth, 0, 1, PROMPT_TILE)
    ys, ps, cs, fs = _run_stream(
        x_sample, (state_pool, state_conv, state_ffn), prm, depth, PAST_LEN,
        x_sample.shape[0], x_sample.shape[1])
    return yp, ys, pp, cp, fp, ps, cs, fs
```
